```python
import math
import jax, jax.numpy as jnp
from jax import lax
import numpy as np

D_MODEL = 4096
BATCH = 2
SEQ = 4096
DEPTH = 1
DEC_BATCH = 128
DEC_SEQ = 4
PAST_LEN = 2048
PAGE_SIZE = 128

N_HEADS = 8
HEAD_DIM = 128
V_DIM = 2 * HEAD_DIM
ATTN_W = N_HEADS * 2 * HEAD_DIM
POOL_WINDOWS = (2, 4, 8, 16)
POOL_GROUPS = len(POOL_WINDOWS)
POOL_GC = D_MODEL // 16
POOL_W = POOL_GROUPS * POOL_GC
POOL_STATE = max(POOL_WINDOWS) - 1
N_MEM = 256
X_HEADS = 4
X_HEAD_DIM = D_MODEL // 16
X_W = X_HEADS * X_HEAD_DIM
N_BRANCH = 3
D_FF = 11008
CONV_W = 3
ROPE_THETA = 10000.0
Q_BLOCK = 128
EPS = 1e-6
NEG = -1e30
POOL_OFF = 3 * ATTN_W
XQ_OFF = POOL_OFF + POOL_W
GATE_OFF = XQ_OFF + X_W
IN_W = GATE_OFF + N_BRANCH * D_MODEL

kernel_name = 'hybrid_diffattn_pool_memory_decoder'


def rms_norm(x, w):
    xf = x.astype(jnp.float32)
    y = xf * lax.rsqrt(jnp.mean(xf * xf, axis=-1, keepdims=True) + EPS)
    return (y * w.astype(jnp.float32)).astype(x.dtype)


def rotary(x, pos):
    half = HEAD_DIM // 2
    inv = ROPE_THETA ** (-jnp.arange(half, dtype=jnp.float32) / half)
    ang = pos.astype(jnp.float32)[:, None] * inv[None, :]
    cos = jnp.cos(ang)[:, None, None, :]
    sin = jnp.sin(ang)[:, None, None, :]
    xf = x.astype(jnp.float32)
    x1, x2 = xf[..., :half], xf[..., half:]
    return jnp.concatenate([x1 * cos - x2 * sin, x2 * cos + x1 * sin], axis=-1).astype(x.dtype)


def diff_weights(s, lam):
    p = jax.nn.softmax(s, axis=-1)
    return p[..., 0, :, :] - lam * p[..., 1, :, :]


def diff_attn_prompt(q, k, v, lam):
    B, S = q.shape[0], q.shape[1]
    nb = S // Q_BLOCK
    scale = HEAD_DIM ** -0.5
    qb = q.reshape(B, nb, Q_BLOCK, N_HEADS, 2, HEAD_DIM).transpose(1, 0, 2, 3, 4, 5)
    kpos = jnp.arange(S)

    def block(args):
        qi, i = args
        s = jnp.einsum('bqhcd,bkhcd->bhcqk', qi, k).astype(jnp.float32) * scale
        qpos = i * Q_BLOCK + jnp.arange(Q_BLOCK)
        s = jnp.where(kpos[None, :] <= qpos[:, None], s, NEG)
        w = diff_weights(s, lam).astype(v.dtype)
        return jnp.einsum('bhqk,bkhe->bqhe', w, v)

    o = lax.map(block, (qb, jnp.arange(nb)))
    return o.transpose(1, 0, 2, 3, 4).reshape(B, S, N_HEADS, V_DIM)


def diff_attn_sample(q, k_new, v_new, cache_k, cache_v, page_table, lam):
    T = q.shape[1]
    past = page_table.shape[1] * cache_k.shape[1]
    scale = HEAD_DIM ** -0.5
    causal = jnp.arange(T)[None, :] <= jnp.arange(T)[:, None]

    def one(args):
        qi, ki, vi, pages = args
        kp = cache_k[pages].reshape(past, N_HEADS, 2, HEAD_DIM)
        vp = cache_v[pages].reshape(past, N_HEADS, V_DIM)
        s_past = jnp.einsum('qhcd,khcd->hcqk', qi, kp).astype(jnp.float32) * scale
        s_new = jnp.einsum('qhcd,khcd->hcqk', qi, ki).astype(jnp.float32) * scale
        s = jnp.concatenate([s_past, jnp.where(causal, s_new, NEG)], axis=-1)
        w = diff_weights(s, lam).astype(vi.dtype)
        vals = jnp.concatenate([vp.astype(vi.dtype), vi], axis=0)
        return jnp.einsum('hqk,khe->qhe', w, vals)

    return lax.map(one, (q, k_new, v_new, page_table))


def mem_kv(mem, mem_norm_w, w_mem_k, w_mem_v, xk_norm_w):
    Bn, M, _ = mem.shape
    m = rms_norm(mem, mem_norm_w)
    k = rms_norm((m @ w_mem_k).reshape(Bn, M, X_HEADS, X_HEAD_DIM), xk_norm_w)
    v = (m @ w_mem_v).reshape(Bn, M, X_HEADS, X_HEAD_DIM)
    return k, v


def cross_attn(xq, mk, mv):
    s = jnp.einsum('bthd,bmhd->bhtm', xq, mk.astype(xq.dtype)).astype(jnp.float32) * (X_HEAD_DIM ** -0.5)
    p = jax.nn.softmax(s, axis=-1).astype(xq.dtype)
    return jnp.einsum('bhtm,bmhd->bthd', p, mv.astype(xq.dtype))


def pool_mixer(zbuf, pos0, n_out, pool_w, pool_scale):
    L = zbuf.shape[1]
    zf = zbuf.astype(jnp.float32)
    csum = jnp.concatenate([jnp.zeros_like(zf[:, :1]), jnp.cumsum(zf, axis=1)], axis=1)
    rows = jnp.arange(L - n_out, L)
    hi = rows + 1
    pos = pos0 + rows
    x_self = zf[:, L - n_out:]
    outs = []
    for g, w in enumerate(POOL_WINDOWS):
        sl = slice(g * POOL_GC, (g + 1) * POOL_GC)
        lo = jnp.maximum(hi - w, 0)
        cnt = jnp.minimum(w, pos + 1).astype(jnp.float32)[None, :, None]
        d = (csum[:, hi, sl] - csum[:, lo, sl]) / cnt - x_self[:, :, sl]
        outs.append(jnp.einsum('btc,cd->btd', d.astype(zbuf.dtype), pool_w[g]))
    return jnp.concatenate(outs, axis=-1) * pool_scale


def layer(x, pos, attn_fn, mem_k, mem_v, pool_prev, conv_prev, lw, lam_init):
    Bn, T, _ = x.shape
    h = rms_norm(x, lw['attn_norm_w'])
    z = h @ lw['w_in']
    q = z[..., :ATTN_W].reshape(Bn, T, N_HEADS, 2, HEAD_DIM)
    k = z[..., ATTN_W:2 * ATTN_W].reshape(Bn, T, N_HEADS, 2, HEAD_DIM)
    v = z[..., 2 * ATTN_W:POOL_OFF].reshape(Bn, T, N_HEADS, V_DIM)
    pz = z[..., POOL_OFF:XQ_OFF]
    xq = z[..., XQ_OFF:GATE_OFF].reshape(Bn, T, X_HEADS, X_HEAD_DIM)
    gates = jax.nn.sigmoid(z[..., GATE_OFF:].astype(jnp.float32)).astype(x.dtype).reshape(Bn, T, N_BRANCH, D_MODEL)
    q = rotary(rms_norm(q, lw['q_norm_w']), pos)
    k = rotary(rms_norm(k, lw['k_norm_w']), pos)
    xq = rms_norm(xq, lw['xq_norm_w'])
    a = attn_fn(q, k, v)
    a = (rms_norm(a, lw['subln_w']) * (1.0 - lam_init)).reshape(Bn, T, ATTN_W)
    pbuf = jnp.concatenate([pool_prev.astype(pz.dtype), pz], axis=1)
    pool_o = pool_mixer(pbuf, pos[0] - pool_prev.shape[1], T, lw['pool_w'], lw['pool_scale'])
    c = cross_attn(xq, mem_k, mem_v).reshape(Bn, T, X_W)
    m = (gates[:, :, 0] * (a @ lw['w_branch_attn'])
         + gates[:, :, 1] * (pool_o @ lw['w_branch_pool'])
         + gates[:, :, 2] * (c @ lw['w_branch_cross']))
    x = x + m @ lw['w_out']
    h2 = rms_norm(x, lw['ffn_norm_w'])
    g = h2 @ lw['w_gate']
    u = h2 @ lw['w_up']
    gbuf = jnp.concatenate([conv_prev.astype(g.dtype), g], axis=1)
    gc = lw['conv_b'] + gbuf[:, 0:T] * lw['conv_w'][0]
    for i in range(1, CONV_W):
        gc = gc + gbuf[:, i:i + T] * lw['conv_w'][i]
    x = x + (jax.nn.silu(gc) * u) @ lw['w_down']
    k_rows = k.reshape(Bn, T, N_HEADS, 2 * HEAD_DIM)
    return x, k_rows, v, pbuf[:, -POOL_STATE:], gbuf[:, -(CONV_W - 1):]


def setup_inputs(seed: int = 0) -> dict:
    key = jax.random.key(seed)
    ks = list(jax.random.split(key, 48))
    f32 = jnp.float32

    def nrm(shape, scale):
        return jax.random.normal(ks.pop(), shape, f32) * scale

    def gain(shape):
        return 1.0 + 0.02 * jax.random.normal(ks.pop(), shape, f32)

    n_pages = PAST_LEN // PAGE_SIZE
    n_used = DEC_BATCH * n_pages
    n_phys = n_used + max(1, n_used // 4)
    perm = jax.random.permutation(ks.pop(), n_phys)
    page_table = perm[:n_used].reshape(DEC_BATCH, n_pages).astype(jnp.int32)
    inputs = {}
    inputs['x_prompt'] = nrm((BATCH, SEQ, D_MODEL), 1.0)
    inputs['x_sample'] = nrm((DEC_BATCH, DEC_SEQ, D_MODEL), 1.0)
    inputs['cache_k'] = nrm((DEPTH, n_phys, PAGE_SIZE, N_HEADS, 2 * HEAD_DIM), 1.0)
    inputs['cache_v'] = nrm((DEPTH, n_phys, PAGE_SIZE, N_HEADS, V_DIM), 1.0)
    inputs['cache_mem_k'] = nrm((DEPTH, DEC_BATCH, N_MEM, X_HEADS, X_HEAD_DIM), 1.0)
    inputs['cache_mem_v'] = nrm((DEPTH, DEC_BATCH, N_MEM, X_HEADS, X_HEAD_DIM), 1.0)
    inputs['state_pool'] = nrm((DEPTH, DEC_BATCH, POOL_STATE, POOL_W), 1.0)
    inputs['state_conv'] = nrm((DEPTH, DEC_BATCH, CONV_W - 1, D_FF), 1.0)
    inputs['page_table'] = page_table
    inputs['mem_prompt'] = nrm((BATCH, N_MEM, D_MODEL), 1.0)
    inputs['attn_norm_w'] = gain((DEPTH, D_MODEL))
    inputs['w_in'] = nrm((DEPTH, D_MODEL, IN_W), D_MODEL ** -0.5)
    inputs['q_norm_w'] = gain((DEPTH, HEAD_DIM))
    inputs['k_norm_w'] = gain((DEPTH, HEAD_DIM))
    inputs['lambda_q1'] = nrm((DEPTH, HEAD_DIM), 0.1)
    inputs['lambda_k1'] = nrm((DEPTH, HEAD_DIM), 0.1)
    inputs['lambda_q2'] = nrm((DEPTH, HEAD_DIM), 0.1)
    inputs['lambda_k2'] = nrm((DEPTH, HEAD_DIM), 0.1)
    inputs['subln_w'] = gain((DEPTH, V_DIM))
    inputs['pool_w'] = nrm((DEPTH, POOL_GROUPS, POOL_GC, POOL_GC), POOL_GC ** -0.5)
    inputs['pool_scale'] = gain((DEPTH, POOL_W))
    inputs['mem_norm_w'] = gain((DEPTH, D_MODEL))
    inputs['w_mem_k'] = nrm((DEPTH, D_MODEL, X_W), D_MODEL ** -0.5)
    inputs['w_mem_v'] = nrm((DEPTH, D_MODEL, X_W), D_MODEL ** -0.5)
    inputs['xq_norm_w'] = gain((DEPTH, X_HEAD_DIM))
    inputs['xk_norm_w'] = gain((DEPTH, X_HEAD_DIM))
    inputs['w_branch_attn'] = nrm((DEPTH, ATTN_W, D_MODEL), ATTN_W ** -0.5)
    inputs['w_branch_pool'] = nrm((DEPTH, POOL_W, D_MODEL), POOL_W ** -0.5)
    inputs['w_branch_cross'] = nrm((DEPTH, X_W, D_MODEL), X_W ** -0.5)
    inputs['w_out'] = nrm((DEPTH, D_MODEL, D_MODEL), D_MODEL ** -0.5)
    inputs['ffn_norm_w'] = gain((DEPTH, D_MODEL))
    inputs['w_gate'] = nrm((DEPTH, D_MODEL, D_FF), D_MODEL ** -0.5)
    inputs['w_up'] = nrm((DEPTH, D_MODEL, D_FF), D_MODEL ** -0.5)
    inputs['conv_w'] = nrm((DEPTH, CONV_W, D_FF), CONV_W ** -0.5)
    inputs['conv_b'] = nrm((DEPTH, D_FF), 0.01)
    inputs['w_down'] = nrm((DEPTH, D_FF, D_MODEL), D_FF ** -0.5)
    return inputs


def reference(x_prompt, x_sample, cache_k, cache_v, cache_mem_k, cache_mem_v, state_pool, state_conv,
              page_table, mem_prompt, attn_norm_w, w_in, q_norm_w, k_norm_w, lambda_q1, lambda_k1,
              lambda_q2, lambda_k2, subln_w, pool_w, pool_scale, mem_norm_w, w_mem_k, w_mem_v,
              xq_norm_w, xk_norm_w, w_branch_attn, w_branch_pool, w_branch_cross, w_out, ffn_norm_w,
              w_gate, w_up, conv_w, conv_b, w_down):
    B, S, _ = x_prompt.shape
    DB, T, _ = x_sample.shape
    past_len = page_table.shape[1] * cache_k.shape[2]
    pos_p = jnp.arange(S, dtype=jnp.int32)
    pos_s = past_len + jnp.arange(T, dtype=jnp.int32)
    y_prompt, y_sample = x_prompt, x_sample
    kp_l, vp_l, mkp_l, mvp_l, poolp_l, convp_l = [], [], [], [], [], []
    ks_l, vs_l, pools_l, convs_l = [], [], [], []
    for l in range(DEPTH):
        lw = {'attn_norm_w': attn_norm_w[l], 'w_in': w_in[l], 'q_norm_w': q_norm_w[l],
              'k_norm_w': k_norm_w[l], 'subln_w': subln_w[l], 'pool_w': pool_w[l],
              'pool_scale': pool_scale[l], 'xq_norm_w': xq_norm_w[l],
              'w_branch_attn': w_branch_attn[l], 'w_branch_pool': w_branch_pool[l],
              'w_branch_cross': w_branch_cross[l], 'w_out': w_out[l], 'ffn_norm_w': ffn_norm_w[l],
              'w_gate': w_gate[l], 'w_up': w_up[l], 'conv_w': conv_w[l], 'conv_b': conv_b[l],
              'w_down': w_down[l]}
        lam_init = 0.8 - 0.6 * math.exp(-0.3 * l)
        lam = (jnp.exp(jnp.sum(lambda_q1[l].astype(jnp.float32) * lambda_k1[l].astype(jnp.float32)))
               - jnp.exp(jnp.sum(lambda_q2[l].astype(jnp.float32) * lambda_k2[l].astype(jnp.float32)))
               + lam_init)
        mk_p, mv_p = mem_kv(mem_prompt, mem_norm_w[l], w_mem_k[l], w_mem_v[l], xk_norm_w[l])
        y_prompt, k_p, v_p, pool_p, conv_p = layer(
            y_prompt, pos_p, lambda q, k, v: diff_attn_prompt(q, k, v, lam), mk_p, mv_p,
            jnp.zeros((B, 0, POOL_W), x_prompt.dtype), jnp.zeros((B, CONV_W - 1, D_FF), x_prompt.dtype),
            lw, lam_init)
        ck, cv = cache_k[l], cache_v[l]
        y_sample, k_s, v_s, pool_s, conv_s = layer(
            y_sample, pos_s, lambda q, k, v: diff_attn_sample(q, k, v, ck, cv, page_table, lam),
            cache_mem_k[l], cache_mem_v[l], state_pool[l], state_conv[l], lw, lam_init)
        kp_l.append(k_p); vp_l.append(v_p); mkp_l.append(mk_p); mvp_l.append(mv_p)
        poolp_l.append(pool_p); convp_l.append(conv_p)
        ks_l.append(k_s); vs_l.append(v_s); pools_l.append(pool_s); convs_l.append(conv_s)
    new_k_prompt = jnp.stack(kp_l)
    new_v_prompt = jnp.stack(vp_l)
    new_mem_k_prompt = jnp.stack(mkp_l)
    new_mem_v_prompt = jnp.stack(mvp_l)
    new_pool_prompt = jnp.stack(poolp_l)
    new_conv_prompt = jnp.stack(convp_l)
    new_k_sample = jnp.stack(ks_l)
    new_v_sample = jnp.stack(vs_l)
    new_pool_sample = jnp.stack(pools_l)
    new_conv_sample = jnp.stack(convs_l)
    return (y_prompt, y_sample, new_k_prompt, new_v_prompt, new_mem_k_prompt, new_mem_v_prompt,
            new_pool_prompt, new_conv_prompt, new_k_sample, new_v_sample, new_pool_sample, new_conv_sample)
```

```python
import functools
import math

import jax
import jax.numpy as jnp
from jax import lax
from jax.experimental import pallas as pl
from jax.experimental.pallas import tpu as pltpu

F32 = jnp.float32
BF16 = jnp.bfloat16

N_HEADS = 8
HEAD_DIM = 128
V_DIM = 2 * HEAD_DIM
ATTN_W = N_HEADS * V_DIM
POOL_WINDOWS = (2, 4, 8, 16)
POOL_GC = 256
POOL_W = len(POOL_WINDOWS) * POOL_GC
X_HEADS = 4
X_HEAD_DIM = 256
X_W = X_HEADS * X_HEAD_DIM
CONV_W = 3
ROPE_THETA = 10000.0
EPS = 1e-6
NEG = -1e30

SUBLANES = 8
BF16_ROWS = 16
LANES = 128
PAGES_PER_STEP = 4
XATTN_GROUP = 4
MIB = 2 ** 20


def _params(sem, vmem_mib):
    return pltpu.CompilerParams(dimension_semantics=sem, vmem_limit_bytes=vmem_mib * MIB)


def _row_tile(m, cap):
    t = min(m, cap)
    assert m % t == 0, (m, t)
    return t


def _dot(a, b):
    return jnp.dot(a, b, preferred_element_type=F32)


def _dot_nt(a, b):
    return lax.dot_general(a, b, (((1,), (1,)), ((), ())), preferred_element_type=F32)


def _rmsnorm_body(x_ref, w_ref, o_ref):
    x = x_ref[...]
    ms = jnp.mean(x * x, axis=-1, keepdims=True)
    o_ref[...] = (x * lax.rsqrt(ms + EPS) * w_ref[...]).astype(o_ref.dtype)


def _rmsnorm(x, w, name):
    m, d = x.shape
    tm = _row_tile(m, 256)
    return pl.pallas_call(
        _rmsnorm_body,
        out_shape=jax.ShapeDtypeStruct((m, d), BF16),
        grid=(m // tm,),
        in_specs=[pl.BlockSpec((tm, d), lambda i: (i, 0)), pl.BlockSpec((1, d), lambda i: (0, 0))],
        out_specs=pl.BlockSpec((tm, d), lambda i: (i, 0)),
        compiler_params=_params(("parallel",), 32),
        name=name,
    )(x, w.reshape(1, d))


def _proj_body(*refs, tn, chunk, rotary, sigmoid):
    x_ref, w_ref = refs[0], refs[1]
    idx = 2
    if chunk:
        nw = refs[idx][...]
        idx += 1
    if rotary:
        cos = refs[idx][...]
        sin = refs[idx + 1][...]
        idx += 2
    outs = refs[idx:]
    z = _dot(x_ref[...], w_ref[...])
    if sigmoid:
        z = jax.nn.sigmoid(z)
    if not chunk:
        for o in outs:
            o[...] = z.astype(o.dtype)
        return
    for c in range(tn // chunk):
        cols = slice(c * chunk, (c + 1) * chunk)
        zc = z[:, cols]
        ms = jnp.mean(zc * zc, axis=-1, keepdims=True)
        y = zc * lax.rsqrt(ms + EPS) * nw
        if rotary:
            y = y * cos + pltpu.roll(y, chunk // 2, 1) * sin
        for o in outs:
            o[:, cols] = y.astype(o.dtype)


def _proj(x, w, col0, ncols, out_dtypes, name, *, tm, norm_w=None, rope=None, rope_rep=1, sigmoid=False,
          tn=512):
    m, k = x.shape
    assert m % tm == 0 and ncols % tn == 0 and col0 % tn == 0
    cb0 = col0 // tn
    chunk = 0 if norm_w is None else norm_w.shape[-1]
    ins = [x, w]
    in_specs = [pl.BlockSpec((tm, k), lambda i, j: (i, 0)),
                pl.BlockSpec((k, tn), lambda i, j: (0, cb0 + j))]
    if chunk:
        ins.append(norm_w.reshape(1, chunk).astype(F32))
        in_specs.append(pl.BlockSpec((1, chunk), lambda i, j: (0, 0)))
    if rope is not None:
        for t in rope:
            ins.append(t)
            in_specs.append(pl.BlockSpec((tm, chunk), lambda i, j: (i % rope_rep, 0)))
    out_shape = [jax.ShapeDtypeStruct((m, ncols), dt) for dt in out_dtypes]
    out_specs = [pl.BlockSpec((tm, tn), lambda i, j: (i, j)) for _ in out_dtypes]
    body = functools.partial(_proj_body, tn=tn, chunk=chunk, rotary=rope is not None, sigmoid=sigmoid)
    res = pl.pallas_call(
        body,
        out_shape=out_shape,
        grid=(m // tm, ncols // tn),
        in_specs=in_specs,
        out_specs=out_specs,
        compiler_params=_params(("parallel", "arbitrary"), 48),
        name=name,
    )(*ins)
    return res


def _lam_value(lq1, lk1, lq2, lk2, lam_init):
    a = jnp.sum(lq1[...] * lk1[...], axis=-1, keepdims=True)
    b = jnp.sum(lq2[...] * lk2[...], axis=-1, keepdims=True)
    return jnp.exp(a) - jnp.exp(b) + lam_init


def _subln(o, sw, lam_init):
    ms = jnp.mean(o * o, axis=-1, keepdims=True)
    return (o * lax.rsqrt(ms + EPS) * sw) * (1.0 - lam_init)


def _attn_body(lq1, lk1, lq2, lk2, sw_ref, q_ref, k_ref, v_ref, o_ref,
               m1, l1, a1, m2, l2, a2, *, tq, lam_init, scale):
    qi = pl.program_id(2)
    q = q_ref[...]
    qs = (q[:, :HEAD_DIM], q[:, HEAD_DIM:])
    stats = ((m1, l1, a1), (m2, l2, a2))
    for m, l, a in stats:
        m[...] = jnp.full(m.shape, NEG, F32)
        l[...] = jnp.zeros(l.shape, F32)
        a[...] = jnp.zeros(a.shape, F32)

    def block(j, mask):
        start = pl.multiple_of(j * tq, tq)
        kb = k_ref[pl.ds(start, tq), :]
        vb = v_ref[pl.ds(start, tq), :]
        for c, (m, l, a) in enumerate(stats):
            s = _dot_nt(qs[c], kb[:, c * HEAD_DIM:(c + 1) * HEAD_DIM]) * scale
            if mask is not None:
                s = jnp.where(mask, s, NEG)
            m_old = m[...]
            m_new = jnp.maximum(m_old, jnp.max(s, axis=-1, keepdims=True))
            alpha = jnp.exp(m_old - m_new)
            p = jnp.exp(s - m_new)
            l[...] = alpha * l[...] + jnp.sum(p, axis=-1, keepdims=True)
            a[...] = alpha * a[...] + _dot(p.astype(BF16), vb)
            m[...] = m_new

    def body(j, carry):
        block(j, None)
        return carry

    lax.fori_loop(0, qi, body, 0)
    row = lax.broadcasted_iota(jnp.int32, (tq, tq), 0)
    col = lax.broadcasted_iota(jnp.int32, (tq, tq), 1)
    block(qi, col <= row)

    lam = _lam_value(lq1, lk1, lq2, lk2, lam_init)
    o = a1[...] / l1[...] - lam * (a2[...] / l2[...])
    o_ref[...] = _subln(o, sw_ref[...], lam_init).astype(o_ref.dtype)


def _attn_prompt(q, k, v, lam_vecs, subln_w, b, s, lam_init):
    tq = _row_tile(s, 512)
    nq = s // tq
    vec = pl.BlockSpec((1, HEAD_DIM), lambda bi, h, i: (0, 0))
    body = functools.partial(_attn_body, tq=tq, lam_init=lam_init, scale=HEAD_DIM ** -0.5)
    return pl.pallas_call(
        body,
        out_shape=jax.ShapeDtypeStruct((b * s, ATTN_W), BF16),
        grid=(b, N_HEADS, nq),
        in_specs=[vec, vec, vec, vec,
                  pl.BlockSpec((1, V_DIM), lambda bi, h, i: (0, 0)),
                  pl.BlockSpec((tq, V_DIM), lambda bi, h, i: (bi * nq + i, h)),
                  pl.BlockSpec((s, V_DIM), lambda bi, h, i: (bi, h)),
                  pl.BlockSpec((s, V_DIM), lambda bi, h, i: (bi, h))],
        out_specs=pl.BlockSpec((tq, V_DIM), lambda bi, h, i: (bi * nq + i, h)),
        scratch_shapes=[pltpu.VMEM((tq, 1), F32), pltpu.VMEM((tq, 1), F32), pltpu.VMEM((tq, V_DIM), F32),
                        pltpu.VMEM((tq, 1), F32), pltpu.VMEM((tq, 1), F32), pltpu.VMEM((tq, V_DIM), F32)],
        compiler_params=_params(("parallel", "parallel", "arbitrary"), 40),
        name="attn_prompt",
    )(*lam_vecs, subln_w.reshape(1, V_DIM), q, k, v)


def _sattn_body(pt_ref, lq1, lk1, lq2, lk2, sw_ref, q_ref, kn_ref, vn_ref, *rest, npg, lam_init, scale):
    del pt_ref
    k_refs = rest[:npg]
    v_refs = rest[npg:2 * npg]
    o_ref = rest[2 * npg]
    m_ref, l_ref, acc_ref, kpad_ref, vpad_ref = rest[2 * npg + 1:]
    c = pl.program_id(1)
    rows_per_head = 2 * SUBLANES

    def process(kpages, vpages, mask):
        for h in range(N_HEADS):
            rows = slice(h * rows_per_head, (h + 1) * rows_per_head)
            cols = slice(h * V_DIM, (h + 1) * V_DIM)
            qh = q_ref[rows, :].astype(BF16)
            s = jnp.concatenate([_dot_nt(qh, kp[:, cols]) for kp in kpages], axis=1) * scale
            if mask is not None:
                s = jnp.where(mask, s, NEG)
            m_old = m_ref[rows, :]
            m_new = jnp.maximum(m_old, jnp.max(s, axis=-1, keepdims=True))
            alpha = jnp.exp(m_old - m_new)
            p = jnp.exp(s - m_new)
            l_ref[rows, :] = alpha * l_ref[rows, :] + jnp.sum(p, axis=-1, keepdims=True)
            pb = p.astype(BF16)
            pg = kpages[0].shape[0]
            pv = _dot(pb[:, :pg], vpages[0][:, cols])
            for i in range(1, len(vpages)):
                pv = pv + _dot(pb[:, i * pg:(i + 1) * pg], vpages[i][:, cols])
            acc_ref[rows, :] = alpha * acc_ref[rows, :] + pv
            m_ref[rows, :] = m_new

    @pl.when(c == 0)
    def _():
        m_ref[...] = jnp.full(m_ref.shape, NEG, F32)
        l_ref[...] = jnp.zeros(l_ref.shape, F32)
        acc_ref[...] = jnp.zeros(acc_ref.shape, F32)
        kpad_ref[...] = jnp.zeros(kpad_ref.shape, BF16)
        vpad_ref[...] = jnp.zeros(vpad_ref.shape, BF16)
        kpad_ref[0:BF16_ROWS, :] = kn_ref[...]
        vpad_ref[0:BF16_ROWS, :] = vn_ref[...]
        shape = (rows_per_head, kpad_ref.shape[0])
        t = lax.broadcasted_iota(jnp.int32, shape, 0) % SUBLANES
        col = lax.broadcasted_iota(jnp.int32, shape, 1)
        process([kpad_ref[...]], [vpad_ref[...]], col <= t)

    process([r[...].astype(BF16) for r in k_refs], [r[...].astype(BF16) for r in v_refs], None)

    @pl.when(c == pl.num_programs(1) - 1)
    def _():
        lam = _lam_value(lq1, lk1, lq2, lk2, lam_init)
        sw = sw_ref[...]
        for h in range(N_HEADS):
            r1 = slice(h * rows_per_head, h * rows_per_head + SUBLANES)
            r2 = slice(h * rows_per_head + SUBLANES, (h + 1) * rows_per_head)
            o = acc_ref[r1, :] / l_ref[r1, :] - lam * (acc_ref[r2, :] / l_ref[r2, :])
            o_ref[h * SUBLANES:(h + 1) * SUBLANES, :] = _subln(o, sw, lam_init)


def _attn_sample(q32, k_bf, v_bf, cache_k, cache_v, page_table, lam_vecs, subln_w, db, t, lam_init):
    assert t <= SUBLANES
    n_phys, page, _, _ = cache_k.shape
    n_pages = page_table.shape[1]
    npg = math.gcd(PAGES_PER_STEP, n_pages)
    ck = cache_k.reshape(n_phys, page, ATTN_W)
    cv = cache_v.reshape(n_phys, page, ATTN_W)
    q5 = q32.reshape(db, t, N_HEADS, 2, HEAD_DIM).transpose(0, 2, 3, 1, 4)
    q5 = jnp.pad(q5, ((0, 0), (0, 0), (0, 0), (0, SUBLANES - t), (0, 0)))
    eye = jnp.eye(2, dtype=F32)
    qbd = (q5[:, :, :, :, None, :] * eye[None, None, :, None, :, None]).reshape(db, N_HEADS * 2 * SUBLANES, V_DIM)
    kn = jnp.pad(k_bf.reshape(db, t, ATTN_W), ((0, 0), (0, BF16_ROWS - t), (0, 0)))
    vn = jnp.pad(v_bf.reshape(db, t, ATTN_W), ((0, 0), (0, BF16_ROWS - t), (0, 0)))
    nrows = N_HEADS * 2 * SUBLANES

    def page_spec(i):
        return pl.BlockSpec((None, page, ATTN_W), lambda b, c, pt: (pt[b, c * npg + i], 0, 0))

    vec = pl.BlockSpec((1, HEAD_DIM), lambda b, c, pt: (0, 0))
    in_specs = [vec, vec, vec, vec,
                pl.BlockSpec((1, V_DIM), lambda b, c, pt: (0, 0)),
                pl.BlockSpec((None, nrows, V_DIM), lambda b, c, pt: (b, 0, 0)),
                pl.BlockSpec((None, BF16_ROWS, ATTN_W), lambda b, c, pt: (b, 0, 0)),
                pl.BlockSpec((None, BF16_ROWS, ATTN_W), lambda b, c, pt: (b, 0, 0))]
    in_specs += [page_spec(i) for i in range(npg)] * 2
    body = functools.partial(_sattn_body, npg=npg, lam_init=lam_init, scale=HEAD_DIM ** -0.5)
    out = pl.pallas_call(
        body,
        out_shape=jax.ShapeDtypeStruct((db, N_HEADS * SUBLANES, V_DIM), F32),
        grid_spec=pltpu.PrefetchScalarGridSpec(
            num_scalar_prefetch=1,
            grid=(db, n_pages // npg),
            in_specs=in_specs,
            out_specs=pl.BlockSpec((None, N_HEADS * SUBLANES, V_DIM), lambda b, c, pt: (b, 0, 0)),
            scratch_shapes=[pltpu.VMEM((nrows, 1), F32), pltpu.VMEM((nrows, 1), F32),
                            pltpu.VMEM((nrows, V_DIM), F32),
                            pltpu.VMEM((page, ATTN_W), BF16), pltpu.VMEM((page, ATTN_W), BF16)]),
        compiler_params=_params(("arbitrary", "arbitrary"), 40),
        name="attn_sample",
    )(page_table, *lam_vecs, subln_w.reshape(1, V_DIM), qbd, kn, vn, *([ck] * npg), *([cv] * npg))
    out = out.reshape(db, N_HEADS, SUBLANES, V_DIM)[:, :, :t]
    return out.transpose(0, 2, 1, 3).reshape(db * t, ATTN_W)


def _xattn_prompt_body(q_ref, k_ref, v_ref, o_ref, *, scale):
    s = _dot_nt(q_ref[...], k_ref[...]) * scale
    m = jnp.max(s, axis=-1, keepdims=True)
    p = jnp.exp(s - m)
    p = p / jnp.sum(p, axis=-1, keepdims=True)
    o_ref[...] = _dot(p.astype(BF16), v_ref[...]).astype(o_ref.dtype)


def _xattn_prompt(xq, mk, mv, b, s, n_mem):
    tq = _row_tile(s, 1024)
    nq = s // tq
    body = functools.partial(_xattn_prompt_body, scale=X_HEAD_DIM ** -0.5)
    return pl.pallas_call(
        body,
        out_shape=jax.ShapeDtypeStruct((b * s, X_W), BF16),
        grid=(b, X_HEADS, nq),
        in_specs=[pl.BlockSpec((tq, X_HEAD_DIM), lambda bi, h, i: (bi * nq + i, h)),
                  pl.BlockSpec((n_mem, X_HEAD_DIM), lambda bi, h, i: (bi, h)),
                  pl.BlockSpec((n_mem, X_HEAD_DIM), lambda bi, h, i: (bi, h))],
        out_specs=pl.BlockSpec((tq, X_HEAD_DIM), lambda bi, h, i: (bi * nq + i, h)),
        compiler_params=_params(("parallel", "parallel", "parallel"), 32),
        name="xattn_prompt",
    )(xq, mk, mv)


def _xattn_sample_body(q_ref, k_ref, v_ref, o_ref, *, t, n_mem, scale):
    g = k_ref.shape[0]
    rows = q_ref.shape[0]
    shape = (rows, g * n_mem)
    same = (lax.broadcasted_iota(jnp.int32, shape, 0) // t) == (lax.broadcasted_iota(jnp.int32, shape, 1) // n_mem)
    for h in range(X_HEADS):
        cols = slice(h * X_HEAD_DIM, (h + 1) * X_HEAD_DIM)
        kh = k_ref[:, :, cols].reshape(g * n_mem, X_HEAD_DIM).astype(BF16)
        vh = v_ref[:, :, cols].reshape(g * n_mem, X_HEAD_DIM).astype(BF16)
        s = jnp.where(same, _dot_nt(q_ref[:, cols], kh) * scale, NEG)
        m = jnp.max(s, axis=-1, keepdims=True)
        p = jnp.exp(s - m)
        p = p / jnp.sum(p, axis=-1, keepdims=True)
        o_ref[:, cols] = _dot(p.astype(BF16), vh).astype(o_ref.dtype)


def _xattn_sample(xq, mem_k, mem_v, db, t):
    n_mem = mem_k.shape[1]
    g = math.gcd(XATTN_GROUP, db)
    mk = mem_k.reshape(db, n_mem, X_W)
    mv = mem_v.reshape(db, n_mem, X_W)
    body = functools.partial(_xattn_sample_body, t=t, n_mem=n_mem, scale=X_HEAD_DIM ** -0.5)
    return pl.pallas_call(
        body,
        out_shape=jax.ShapeDtypeStruct((db * t, X_W), BF16),
        grid=(db // g,),
        in_specs=[pl.BlockSpec((g * t, X_W), lambda i: (i, 0)),
                  pl.BlockSpec((g, n_mem, X_W), lambda i: (i, 0, 0)),
                  pl.BlockSpec((g, n_mem, X_W), lambda i: (i, 0, 0))],
        out_specs=pl.BlockSpec((g * t, X_W), lambda i: (i, 0)),
        compiler_params=_params(("parallel",), 40),
        name="xattn_sample",
    )(xq, mk, mv)


POOL_HALO = 16


def _pool_prompt_body(x_ref, halo_ref, pw_ref, ps_ref, o_ref, buf, *, tm, tiles_per_seq):
    i = pl.program_id(0)
    first = (i % tiles_per_seq) == 0
    x = x_ref[...]
    buf[0:POOL_HALO, :] = jnp.where(first, 0.0, halo_ref[...])
    buf[POOL_HALO:, :] = x
    pos = (i % tiles_per_seq) * tm + lax.broadcasted_iota(jnp.int32, (tm, 1), 0)
    for g, w in enumerate(POOL_WINDOWS):
        cols = slice(g * POOL_GC, (g + 1) * POOL_GC)
        xs = x[:, cols]
        acc = xs
        for j in range(1, w):
            acc = acc + buf[POOL_HALO - j:POOL_HALO - j + tm, cols]
        cnt = jnp.minimum(w, pos + 1).astype(F32)
        d = acc / cnt - xs
        o_ref[:, cols] = (_dot(d.astype(BF16), pw_ref[g]) * ps_ref[:, cols]).astype(o_ref.dtype)


def _pool_prompt(pz, pool_w, pool_scale, s):
    m = pz.shape[0]
    tm = _row_tile(s, 512)
    hb = tm // POOL_HALO
    body = functools.partial(_pool_prompt_body, tm=tm, tiles_per_seq=s // tm)
    return pl.pallas_call(
        body,
        out_shape=jax.ShapeDtypeStruct((m, POOL_W), BF16),
        grid=(m // tm,),
        in_specs=[pl.BlockSpec((tm, POOL_W), lambda i: (i, 0)),
                  pl.BlockSpec((POOL_HALO, POOL_W), lambda i: (jnp.maximum(i * hb - 1, 0), 0)),
                  pl.BlockSpec(pool_w.shape, lambda i: (0, 0, 0)),
                  pl.BlockSpec((1, POOL_W), lambda i: (0, 0))],
        out_specs=pl.BlockSpec((tm, POOL_W), lambda i: (i, 0)),
        scratch_shapes=[pltpu.VMEM((tm + POOL_HALO, POOL_W), F32)],
        compiler_params=_params(("parallel",), 32),
        name="pool_prompt",
    )(pz, pz, pool_w, pool_scale.reshape(1, POOL_W))


def _pool_sample_body(st_ref, pz_ref, pw_ref, ps_ref, o_ref, *, n_state, t, past_len):
    def row(idx, cols):
        return st_ref[idx][:, cols] if idx < n_state else pz_ref[idx - n_state][:, cols]

    for g, w in enumerate(POOL_WINDOWS):
        cols = slice(g * POOL_GC, (g + 1) * POOL_GC)
        ds = []
        for ti in range(t):
            hi = n_state + ti + 1
            lo = max(hi - w, 0)
            acc = row(hi - 1, cols)
            for idx in range(hi - 2, lo - 1, -1):
                acc = acc + row(idx, cols)
            cnt = float(min(w, past_len + ti + 1))
            ds.append(acc / cnt - row(hi - 1, cols))
        d = jnp.concatenate(ds, axis=0)
        o_ref[:, cols] = (_dot(d.astype(BF16), pw_ref[g]) * ps_ref[:, cols]).astype(o_ref.dtype)


def _pool_sample(pz, state_pool, pool_w, pool_scale, db, t, past_len):
    n_state = state_pool.shape[1]
    st = state_pool.transpose(1, 0, 2)
    pzt = pz.reshape(db, t, POOL_W).transpose(1, 0, 2)
    body = functools.partial(_pool_sample_body, n_state=n_state, t=t, past_len=past_len)
    out = pl.pallas_call(
        body,
        out_shape=jax.ShapeDtypeStruct((t * db, POOL_W), BF16),
        grid=(1,),
        in_specs=[pl.BlockSpec(st.shape, lambda i: (0, 0, 0)),
                  pl.BlockSpec(pzt.shape, lambda i: (0, 0, 0)),
                  pl.BlockSpec(pool_w.shape, lambda i: (0, 0, 0)),
                  pl.BlockSpec((1, POOL_W), lambda i: (0, 0))],
        out_specs=pl.BlockSpec((t * db, POOL_W), lambda i: (0, 0)),
        compiler_params=_params(("arbitrary",), 40),
        name="pool_sample",
    )(st, pzt, pool_w, pool_scale.reshape(1, POOL_W))
    return out.reshape(t, db, POOL_W).transpose(1, 0, 2).reshape(db * t, POOL_W)


def _merge_body(a_ref, p_ref, c_ref, wa_ref, wp_ref, wc_ref, g0_ref, g1_ref, g2_ref, o_ref):
    m = (g0_ref[...] * _dot(a_ref[...], wa_ref[...])
         + g1_ref[...] * _dot(p_ref[...], wp_ref[...])
         + g2_ref[...] * _dot(c_ref[...], wc_ref[...]))
    o_ref[...] = m.astype(o_ref.dtype)


def _merge(a, po, c, gates, wa, wp, wc, d, tm):
    m = a.shape[0]
    assert m % tm == 0
    tn = 256
    nj = d // tn
    row = lambda k: pl.BlockSpec((tm, k), lambda i, j: (i, 0))
    wcol = lambda k: pl.BlockSpec((k, tn), lambda i, j: (0, j))
    gate = lambda r: pl.BlockSpec((tm, tn), lambda i, j: (i, r * nj + j))
    return pl.pallas_call(
        _merge_body,
        out_shape=jax.ShapeDtypeStruct((m, d), BF16),
        grid=(m // tm, nj),
        in_specs=[row(ATTN_W), row(POOL_W), row(X_W), wcol(ATTN_W), wcol(POOL_W), wcol(X_W),
                  gate(0), gate(1), gate(2)],
        out_specs=pl.BlockSpec((tm, tn), lambda i, j: (i, j)),
        compiler_params=_params(("parallel", "arbitrary"), 48),
        name="merge",
    )(a, po, c, wa, wp, wc, gates, gates, gates)


def _resid_body(x_ref, w_ref, r_ref, o_ref):
    o_ref[...] = r_ref[...] + _dot(x_ref[...], w_ref[...])


def _resid_proj(x, w, resid, name, *, tm, tn):
    m, k = x.shape
    n = w.shape[1]
    assert m % tm == 0 and n % tn == 0
    return pl.pallas_call(
        _resid_body,
        out_shape=jax.ShapeDtypeStruct((m, n), F32),
        grid=(m // tm, n // tn),
        in_specs=[pl.BlockSpec((tm, k), lambda i, j: (i, 0)),
                  pl.BlockSpec((k, tn), lambda i, j: (0, j)),
                  pl.BlockSpec((tm, tn), lambda i, j: (i, j))],
        out_specs=pl.BlockSpec((tm, tn), lambda i, j: (i, j)),
        compiler_params=_params(("parallel", "arbitrary"), 48),
        name=name,
    )(x, w, resid)


CONV_HALO = SUBLANES


def _ffn_a_body(*refs, tm, tiles_per_seq, seq_len, has_prev):
    h_ref, wg_ref, wu_ref, cw_ref, cb_ref = refs[:5]
    idx = 5
    if has_prev:
        p1_ref, p2_ref = refs[idx:idx + 2]
        idx += 2
    act_ref, g_ref, buf, carry = refs[idx:]
    i = pl.program_id(0)
    j = pl.program_id(1)
    h = h_ref[...]
    g = _dot(h, wg_ref[...])
    u = _dot(h, wu_ref[...])
    if tiles_per_seq:
        first = (i % tiles_per_seq) == 0
        buf[0:CONV_HALO, :] = jnp.where(first, 0.0, carry[j])
    else:
        buf[0:CONV_HALO, :] = jnp.zeros((CONV_HALO, g.shape[1]), F32)
    buf[CONV_HALO:, :] = g
    carry[j] = g[tm - CONV_HALO:, :]
    g1 = buf[CONV_HALO - 1:CONV_HALO - 1 + tm, :]
    g2 = buf[CONV_HALO - 2:CONV_HALO - 2 + tm, :]
    if not tiles_per_seq:
        t = lax.broadcasted_iota(jnp.int32, (tm, 1), 0) % seq_len
        g1 = jnp.where(t >= 1, g1, 0.0)
        g2 = jnp.where(t >= 2, g2, 0.0)
    if has_prev:
        g1 = g1 + p1_ref[...]
        g2 = g2 + p2_ref[...]
    cw = cw_ref[...]
    gc = cb_ref[...] + g2 * cw[0:1, :]
    gc = gc + g1 * cw[1:2, :]
    gc = gc + g * cw[2:3, :]
    act_ref[...] = (jax.nn.silu(gc) * u).astype(act_ref.dtype)
    if tiles_per_seq:
        g_ref[...] = g[tm - CONV_HALO:, :]
    else:
        g_ref[...] = g


def _ffn_a(h2, wg, wu, conv_w, conv_b, seq_len, tm, prev=None):
    m, d = h2.shape
    dff = wg.shape[1]
    tn = 256
    assert dff % tn == 0 and m % tm == 0
    long_seq = seq_len >= tm
    if long_seq:
        assert seq_len % tm == 0 and prev is None
        tiles_per_seq = seq_len // tm
    else:
        assert tm == m and tm % seq_len == 0
        tiles_per_seq = 0
    nj = dff // tn
    ins = [h2, wg, wu, conv_w, conv_b.reshape(1, dff)]
    in_specs = [pl.BlockSpec((tm, d), lambda i, j: (i, 0)),
                pl.BlockSpec((d, tn), lambda i, j: (0, j)),
                pl.BlockSpec((d, tn), lambda i, j: (0, j)),
                pl.BlockSpec((CONV_W, tn), lambda i, j: (0, j)),
                pl.BlockSpec((1, tn), lambda i, j: (0, j))]
    if prev is not None:
        ins += list(prev)
        in_specs += [pl.BlockSpec((tm, tn), lambda i, j: (i, j))] * 2
    if long_seq:
        g_shape = jax.ShapeDtypeStruct((m // tm, CONV_HALO, dff), F32)
        g_spec = pl.BlockSpec((None, CONV_HALO, tn), lambda i, j: (i, 0, j))
    else:
        g_shape = jax.ShapeDtypeStruct((m, dff), F32)
        g_spec = pl.BlockSpec((tm, tn), lambda i, j: (i, j))
    body = functools.partial(_ffn_a_body, tm=tm, tiles_per_seq=tiles_per_seq, seq_len=seq_len,
                             has_prev=prev is not None)
    return pl.pallas_call(
        body,
        out_shape=[jax.ShapeDtypeStruct((m, dff), BF16), g_shape],
        grid=(m // tm, nj),
        in_specs=in_specs,
        out_specs=[pl.BlockSpec((tm, tn), lambda i, j: (i, j)), g_spec],
        scratch_shapes=[pltpu.VMEM((tm + CONV_HALO, tn), F32), pltpu.VMEM((nj, CONV_HALO, tn), F32)],
        compiler_params=_params(("arbitrary", "arbitrary"), 48),
        name="ffn_gate_up",
    )(*ins)


def _rope_tables(pos):
    half = HEAD_DIM // 2
    inv = ROPE_THETA ** (-jnp.arange(half, dtype=F32) / half)
    ang = pos.astype(F32)[:, None] * inv[None, :]
    cos, sin = jnp.cos(ang), jnp.sin(ang)
    return jnp.concatenate([cos, cos], axis=-1), jnp.concatenate([-sin, sin], axis=-1)


def _layer(x2, w, lam_init, *, tm, seq_len, rope, attn_fn, pool_fn, xattn_fn, conv_prev, q_dtype):
    d = x2.shape[1]
    rope_rep = rope[0].shape[0] // tm
    pool_off = 3 * ATTN_W
    xq_off = pool_off + POOL_W
    gate_off = xq_off + X_W
    h = _rmsnorm(x2, w['attn_norm_w'], "attn_norm")
    win = w['w_in']
    (q,) = _proj(h, win, 0, ATTN_W, [q_dtype], "proj_q", tm=tm, norm_w=w['q_norm_w'], rope=rope,
                 rope_rep=rope_rep)
    k32, kbf = _proj(h, win, ATTN_W, ATTN_W, [F32, BF16], "proj_k", tm=tm, norm_w=w['k_norm_w'], rope=rope,
                     rope_rep=rope_rep)
    v32, vbf = _proj(h, win, 2 * ATTN_W, ATTN_W, [F32, BF16], "proj_v", tm=tm)
    (pz,) = _proj(h, win, pool_off, POOL_W, [F32], "proj_pool", tm=tm)
    (xq,) = _proj(h, win, xq_off, X_W, [BF16], "proj_xq", tm=tm, norm_w=w['xq_norm_w'])
    (gates,) = _proj(h, win, gate_off, 3 * d, [F32], "proj_gates", tm=tm, sigmoid=True)
    a = attn_fn(q, kbf, vbf)
    po = pool_fn(pz)
    c = xattn_fn(xq)
    m = _merge(a, po, c, gates, w['w_branch_attn'], w['w_branch_pool'], w['w_branch_cross'], d, tm)
    x1 = _resid_proj(m, w['w_out'], x2, "out_proj", tm=tm, tn=512)
    h2 = _rmsnorm(x1, w['ffn_norm_w'], "ffn_norm")
    act, gl = _ffn_a(h2, w['w_gate'], w['w_up'], w['conv_w'], w['conv_b'], seq_len, tm, conv_prev)
    y = _resid_proj(act, w['w_down'], x1, "down_proj", tm=min(tm, 512), tn=256)
    return y, k32, v32, pz, gl


def kernel(x_prompt, x_sample, cache_k, cache_v, cache_mem_k, cache_mem_v, state_pool, state_conv, page_table, mem_prompt, attn_norm_w, w_in, q_norm_w, k_norm_w, lambda_q1, lambda_k1, lambda_q2, lambda_k2, subln_w, pool_w, pool_scale, mem_norm_w, w_mem_k, w_mem_v, xq_norm_w, xk_norm_w, w_branch_attn, w_branch_pool, w_branch_cross, w_out, ffn_norm_w, w_gate, w_up, conv_w, conv_b, w_down):
    b, s, d = x_prompt.shape
    db, t, _ = x_sample.shape
    depth = w_in.shape[0]
    dff = w_gate.shape[-1]
    n_mem = mem_prompt.shape[1]
    past_len = page_table.shape[1] * cache_k.shape[2]
    n_state = state_pool.shape[2]
    rope_p = _rope_tables(jnp.arange(s, dtype=jnp.int32))
    rope_s = _rope_tables(jnp.tile(past_len + jnp.arange(t, dtype=jnp.int32), db))
    yp = x_prompt.reshape(b * s, d)
    ys = x_sample.reshape(db * t, d)
    outs = [[] for _ in range(10)]
    for l in range(depth):
        lam_init = 0.8 - 0.6 * math.exp(-0.3 * l)
        w = {'attn_norm_w': attn_norm_w[l], 'w_in': w_in[l].astype(BF16), 'q_norm_w': q_norm_w[l],
             'k_norm_w': k_norm_w[l], 'xq_norm_w': xq_norm_w[l],
             'w_branch_attn': w_branch_attn[l].astype(BF16), 'w_branch_pool': w_branch_pool[l].astype(BF16),
             'w_branch_cross': w_branch_cross[l].astype(BF16), 'w_out': w_out[l].astype(BF16),
             'ffn_norm_w': ffn_norm_w[l], 'w_gate': w_gate[l].astype(BF16), 'w_up': w_up[l].astype(BF16),
             'conv_w': conv_w[l], 'conv_b': conv_b[l], 'w_down': w_down[l].astype(BF16)}
        lam_vecs = [v[l].reshape(1, HEAD_DIM).astype(F32) for v in (lambda_q1, lambda_k1, lambda_q2, lambda_k2)]
        pw = pool_w[l].astype(BF16)
        ps = pool_scale[l]
        sw = subln_w[l]

        mn = _rmsnorm(mem_prompt.reshape(b * n_mem, d), mem_norm_w[l], "mem_norm")
        tm_m = _row_tile(b * n_mem, 1024)
        mk32, mkbf = _proj(mn, w_mem_k[l].astype(BF16), 0, X_W, [F32, BF16], "proj_mem_k", tm=tm_m,
                           norm_w=xk_norm_w[l])
        mv32, mvbf = _proj(mn, w_mem_v[l].astype(BF16), 0, X_W, [F32, BF16], "proj_mem_v", tm=tm_m)
        tm_p = _row_tile(s, 1024)
        yp, k_p, v_p, pz_p, gl_p = _layer(
            yp, w, lam_init, tm=tm_p, seq_len=s, rope=rope_p,
            attn_fn=lambda q, k, v: _attn_prompt(q, k, v, lam_vecs, sw, b, s, lam_init),
            pool_fn=lambda pz: _pool_prompt(pz, pw, ps, s),
            xattn_fn=lambda xq: _xattn_prompt(xq, mkbf, mvbf, b, s, n_mem),
            conv_prev=None, q_dtype=BF16)
        tiles_per_seq = s // tm_p
        conv_p = gl_p.reshape(b, tiles_per_seq, CONV_HALO, dff)[:, -1, CONV_HALO - (CONV_W - 1):]

        sc = state_conv[l]
        zrow = jnp.zeros((db, 1, dff), F32)
        p1 = jnp.concatenate([sc[:, 1:2]] + [zrow] * (t - 1), axis=1).reshape(db * t, dff)
        p2 = jnp.concatenate([sc[:, 0:1], sc[:, 1:2]] + [zrow] * (t - 2), axis=1).reshape(db * t, dff)
        ck, cv = cache_k[l], cache_v[l]
        cmk, cmv = cache_mem_k[l], cache_mem_v[l]
        sp = state_pool[l]
        ys, k_s, v_s, pz_s, g_s = _layer(
            ys, w, lam_init, tm=_row_tile(db * t, 1024), seq_len=t, rope=rope_s,
            attn_fn=lambda q, k, v: _attn_sample(q, k, v, ck, cv, page_table, lam_vecs, sw, db, t,
                                                 lam_init).astype(BF16),
            pool_fn=lambda pz: _pool_sample(pz, sp, pw, ps, db, t, past_len),
            xattn_fn=lambda xq: _xattn_sample(xq, cmk, cmv, db, t),
            conv_prev=(p1, p2), q_dtype=F32)
        pool_s = jnp.concatenate([sp, pz_s.reshape(db, t, POOL_W)], axis=1)[:, -n_state:]
        conv_s = jnp.concatenate([sc, g_s.reshape(db, t, dff)], axis=1)[:, -(CONV_W - 1):]

        vals = [k_p.reshape(b, s, N_HEADS, V_DIM), v_p.reshape(b, s, N_HEADS, V_DIM),
                mk32.reshape(b, n_mem, X_HEADS, X_HEAD_DIM), mv32.reshape(b, n_mem, X_HEADS, X_HEAD_DIM),
                pz_p.reshape(b, s, POOL_W)[:, -n_state:], conv_p,
                k_s.reshape(db, t, N_HEADS, V_DIM), v_s.reshape(db, t, N_HEADS, V_DIM), pool_s, conv_s]
        for o, v in zip(outs, vals):
            o.append(v)
    return (yp.reshape(b, s, d), ys.reshape(db, t, d)) + tuple(jnp.stack(o) for o in outs)
```

```python
import functools
import math

import jax
import jax.numpy as jnp
from jax import lax
from jax.experimental import pallas as pl
from jax.experimental.pallas import tpu as pltpu

F32 = jnp.float32
BF16 = jnp.bfloat16

N_HEADS = 8
HEAD_DIM = 128
V_DIM = 2 * HEAD_DIM
ATTN_W = N_HEADS * V_DIM
POOL_WINDOWS = (2, 4, 8, 16)
POOL_GC = 256
POOL_W = len(POOL_WINDOWS) * POOL_GC
X_HEADS = 4
X_HEAD_DIM = 256
X_W = X_HEADS * X_HEAD_DIM
CONV_W = 3
ROPE_THETA = 10000.0
EPS = 1e-6
NEG = -1e30
LOG2_E = math.log2(math.e)

SUBLANES = 8
BF16_ROWS = 16
LANES = 128
PAGES_PER_STEP = 8
XATTN_GROUP = 4
MIB = 2 ** 20


def _params(sem, vmem_mib):
    return pltpu.CompilerParams(dimension_semantics=sem, vmem_limit_bytes=vmem_mib * MIB)


def _row_tile(m, cap):
    t = min(m, cap)
    assert m % t == 0, (m, t)
    return t


def _dot(a, b):
    return jnp.dot(a, b, preferred_element_type=F32)


def _dot_nt(a, b):
    return lax.dot_general(a, b, (((1,), (1,)), ((), ())), preferred_element_type=F32)


def _rmsnorm_body(x_ref, w_ref, o_ref):
    x = x_ref[...]
    ms = jnp.mean(x * x, axis=-1, keepdims=True)
    o_ref[...] = (x * lax.rsqrt(ms + EPS) * w_ref[...]).astype(o_ref.dtype)


def _rmsnorm(x, w, name):
    m, d = x.shape
    tm = _row_tile(m, 256)
    return pl.pallas_call(
        _rmsnorm_body,
        out_shape=jax.ShapeDtypeStruct((m, d), BF16),
        grid=(m // tm,),
        in_specs=[pl.BlockSpec((tm, d), lambda i: (i, 0)), pl.BlockSpec((1, d), lambda i: (0, 0))],
        out_specs=pl.BlockSpec((tm, d), lambda i: (i, 0)),
        compiler_params=_params(("parallel",), 32),
        name=name,
    )(x, w.reshape(1, d))


def _proj_body(*refs, tn, chunk, rotary, sigmoid):
    x_ref, w_ref = refs[0], refs[1]
    idx = 2
    if chunk:
        nw = refs[idx][...]
        idx += 1
    if rotary:
        cos = refs[idx][...]
        sin = refs[idx + 1][...]
        idx += 2
    outs = refs[idx:]
    z = _dot(x_ref[...], w_ref[...])
    if sigmoid:
        z = jax.nn.sigmoid(z)
    if not chunk:
        for o in outs:
            o[...] = z.astype(o.dtype)
        return
    for c in range(tn // chunk):
        cols = slice(c * chunk, (c + 1) * chunk)
        zc = z[:, cols]
        ms = jnp.mean(zc * zc, axis=-1, keepdims=True)
        y = zc * lax.rsqrt(ms + EPS) * nw
        if rotary:
            y = y * cos + pltpu.roll(y, chunk // 2, 1) * sin
        for o in outs:
            o[:, cols] = y.astype(o.dtype)


def _proj(x, w, col0, ncols, out_dtypes, name, *, tm, norm_w=None, rope=None, rope_rep=1, sigmoid=False,
          tn=512):
    m, k = x.shape
    assert m % tm == 0 and ncols % tn == 0 and col0 % tn == 0
    cb0 = col0 // tn
    chunk = 0 if norm_w is None else norm_w.shape[-1]
    ins = [x, w]
    in_specs = [pl.BlockSpec((tm, k), lambda i, j: (i, 0)),
                pl.BlockSpec((k, tn), lambda i, j: (0, cb0 + j))]
    if chunk:
        ins.append(norm_w.reshape(1, chunk).astype(F32))
        in_specs.append(pl.BlockSpec((1, chunk), lambda i, j: (0, 0)))
    if rope is not None:
        for t in rope:
            ins.append(t)
            in_specs.append(pl.BlockSpec((tm, chunk), lambda i, j: (i % rope_rep, 0)))
    out_shape = [jax.ShapeDtypeStruct((m, ncols), dt) for dt in out_dtypes]
    out_specs = [pl.BlockSpec((tm, tn), lambda i, j: (i, j)) for _ in out_dtypes]
    body = functools.partial(_proj_body, tn=tn, chunk=chunk, rotary=rope is not None, sigmoid=sigmoid)
    res = pl.pallas_call(
        body,
        out_shape=out_shape,
        grid=(m // tm, ncols // tn),
        in_specs=in_specs,
        out_specs=out_specs,
        compiler_params=_params(("parallel", "arbitrary"), 48),
        name=name,
    )(*ins)
    return res


def _lam_value(lq1, lk1, lq2, lk2, lam_init):
    a = jnp.sum(lq1[...] * lk1[...], axis=-1, keepdims=True)
    b = jnp.sum(lq2[...] * lk2[...], axis=-1, keepdims=True)
    return jnp.exp(a) - jnp.exp(b) + lam_init


def _subln(o, sw, lam_init):
    ms = jnp.mean(o * o, axis=-1, keepdims=True)
    return (o * lax.rsqrt(ms + EPS) * sw) * (1.0 - lam_init)


def _attn_body(lq1, lk1, lq2, lk2, sw_ref, q_ref, k_ref, v_ref, o_ref,
               m1, l1, a1, m2, l2, a2, *, tq, lam_init, scale):
    qi = pl.program_id(2)
    q = q_ref[...]
    qs = (q[:, :HEAD_DIM], q[:, HEAD_DIM:])
    stats = ((m1, l1, a1), (m2, l2, a2))
    for m, l, a in stats:
        m[...] = jnp.full(m.shape, NEG, F32)
        l[...] = jnp.zeros(l.shape, F32)
        a[...] = jnp.zeros(a.shape, F32)

    def scores(j):
        start = pl.multiple_of(j * tq, tq)
        kb = k_ref[pl.ds(start, tq), :]
        return tuple(_dot_nt(qs[c], kb[:, c * HEAD_DIM:(c + 1) * HEAD_DIM]) for c in range(2))

    def accumulate(j, ss, mask):
        start = pl.multiple_of(j * tq, tq)
        vb = v_ref[pl.ds(start, tq), :]
        for s, (m, l, a) in zip(ss, stats):
            if mask is not None:
                s = jnp.where(mask, s, NEG)
            m_old = m[...]
            m_new = jnp.maximum(m_old, jnp.max(s, axis=-1, keepdims=True))
            alpha = jnp.exp2((m_old - m_new) * (scale * LOG2_E))
            p = jnp.exp2((s - m_new) * (scale * LOG2_E))
            l[...] = alpha * l[...] + jnp.sum(p, axis=-1, keepdims=True)
            a[...] = alpha * a[...] + _dot(p.astype(BF16), vb)
            m[...] = m_new

    def body(j, ss):
        nxt = scores(j + 1)
        accumulate(j, ss, None)
        return nxt

    ss = lax.fori_loop(0, qi, body, scores(0))
    row = lax.broadcasted_iota(jnp.int32, (tq, tq), 0)
    col = lax.broadcasted_iota(jnp.int32, (tq, tq), 1)
    accumulate(qi, ss, col <= row)

    lam = _lam_value(lq1, lk1, lq2, lk2, lam_init)
    o = a1[...] / l1[...] - lam * (a2[...] / l2[...])
    o_ref[...] = _subln(o, sw_ref[...], lam_init).astype(o_ref.dtype)


def _attn_prompt(q, k, v, lam_vecs, subln_w, b, s, lam_init):
    tq = _row_tile(s, 512)
    nq = s // tq
    vec = pl.BlockSpec((1, HEAD_DIM), lambda bi, h, i: (0, 0))
    body = functools.partial(_attn_body, tq=tq, lam_init=lam_init, scale=HEAD_DIM ** -0.5)
    return pl.pallas_call(
        body,
        out_shape=jax.ShapeDtypeStruct((b * s, ATTN_W), BF16),
        grid=(b, N_HEADS, nq),
        in_specs=[vec, vec, vec, vec,
                  pl.BlockSpec((1, V_DIM), lambda bi, h, i: (0, 0)),
                  pl.BlockSpec((tq, V_DIM), lambda bi, h, i: (bi * nq + i, h)),
                  pl.BlockSpec((s, V_DIM), lambda bi, h, i: (bi, h)),
                  pl.BlockSpec((s, V_DIM), lambda bi, h, i: (bi, h))],
        out_specs=pl.BlockSpec((tq, V_DIM), lambda bi, h, i: (bi * nq + i, h)),
        scratch_shapes=[pltpu.VMEM((tq, 1), F32), pltpu.VMEM((tq, 1), F32), pltpu.VMEM((tq, V_DIM), F32),
                        pltpu.VMEM((tq, 1), F32), pltpu.VMEM((tq, 1), F32), pltpu.VMEM((tq, V_DIM), F32)],
        compiler_params=_params(("parallel", "parallel", "arbitrary"), 40),
        name="attn_prompt",
    )(*lam_vecs, subln_w.reshape(1, V_DIM), q, k, v)


def _sattn_body(pt_ref, lq1, lk1, lq2, lk2, sw_ref, q_ref, kn_ref, vn_ref, *rest, npg, t_pad, lam_init, scale):
    del pt_ref
    k_refs = rest[:npg]
    v_refs = rest[npg:2 * npg]
    o_ref = rest[2 * npg]
    m_ref, l_ref, acc_ref = rest[2 * npg + 1:]
    c = pl.program_id(1)
    q = q_ref[...].astype(BF16)
    nrows = q.shape[0]

    def masks(ncols):
        row = lax.broadcasted_iota(jnp.int32, (nrows, ncols), 0)
        col = lax.broadcasted_iota(jnp.int32, (nrows, ncols), 1)
        same_head = (row // (2 * t_pad)) == (col % N_HEADS)
        return same_head, row % t_pad, col // N_HEADS

    def update(kbs, vbs, mask):
        ss = [jnp.where(mask, _dot_nt(q, kb) * scale, NEG) for kb in kbs]
        m_cur = functools.reduce(jnp.maximum, [jnp.max(s, axis=-1, keepdims=True) for s in ss])
        m_old = m_ref[...]
        m_new = jnp.maximum(m_old, m_cur)
        alpha = jnp.exp(m_old - m_new)
        ps = [jnp.exp(s - m_new) for s in ss]
        l_ref[...] = alpha * l_ref[...] + sum(jnp.sum(p, axis=-1, keepdims=True) for p in ps)
        acc_ref[...] = alpha * acc_ref[...] + sum(_dot(p.astype(BF16), vb) for p, vb in zip(ps, vbs))
        m_ref[...] = m_new

    @pl.when(c == 0)
    def _():
        m_ref[...] = jnp.full(m_ref.shape, NEG, F32)
        l_ref[...] = jnp.zeros(l_ref.shape, F32)
        acc_ref[...] = jnp.zeros(acc_ref.shape, F32)
        same_head, step, tok = masks(kn_ref.shape[0])
        update([kn_ref[...]], [vn_ref[...]], same_head & (tok <= step))

    update([r[...].astype(BF16) for r in k_refs], [r[...].astype(BF16) for r in v_refs],
           masks(k_refs[0].shape[0])[0])

    @pl.when(c == pl.num_programs(1) - 1)
    def _():
        lam = _lam_value(lq1, lk1, lq2, lk2, lam_init)
        y = acc_ref[...] / l_ref[...]
        o = y - lam * pltpu.roll(y, nrows - t_pad, 0)
        o_ref[...] = _subln(o, sw_ref[...], lam_init)


def _attn_sample(q32, k_bf, v_bf, cache_k, cache_v, layer, page_table, lam_vecs, subln_w, db, t, lam_init):
    t_pad = SUBLANES // 2
    assert t <= t_pad
    depth, n_phys, page, _, _ = cache_k.shape
    n_pages = page_table.shape[1]
    npg = math.gcd(PAGES_PER_STEP, n_pages)
    ck = cache_k.reshape(depth, n_phys, page * N_HEADS, V_DIM)
    cv = cache_v.reshape(depth, n_phys, page * N_HEADS, V_DIM)
    q5 = q32.reshape(db, t, N_HEADS, 2, HEAD_DIM).transpose(0, 2, 3, 1, 4)
    q5 = jnp.pad(q5, ((0, 0), (0, 0), (0, 0), (0, t_pad - t), (0, 0)))
    eye = jnp.eye(2, dtype=F32)
    nrows = N_HEADS * 2 * t_pad
    qbd = (q5[:, :, :, :, None, :] * eye[None, None, :, None, :, None]).reshape(db, nrows, V_DIM)
    new_tok = LANES // N_HEADS
    pad = ((0, 0), (0, new_tok - t), (0, 0), (0, 0))
    kn = jnp.pad(k_bf.reshape(db, t, N_HEADS, V_DIM), pad).reshape(db, LANES, V_DIM)
    vn = jnp.pad(v_bf.reshape(db, t, N_HEADS, V_DIM), pad).reshape(db, LANES, V_DIM)

    def page_spec(i):
        return pl.BlockSpec((None, None, page * N_HEADS, V_DIM),
                            lambda b, c, pt: (layer, pt[b, c * npg + i], 0, 0))

    vec = pl.BlockSpec((1, HEAD_DIM), lambda b, c, pt: (0, 0))
    per_seq = lambda rows: pl.BlockSpec((None, rows, V_DIM), lambda b, c, pt: (b, 0, 0))
    in_specs = [vec, vec, vec, vec, pl.BlockSpec((1, V_DIM), lambda b, c, pt: (0, 0)),
                per_seq(nrows), per_seq(LANES), per_seq(LANES)]
    in_specs += [page_spec(i) for i in range(npg)] * 2
    body = functools.partial(_sattn_body, npg=npg, t_pad=t_pad, lam_init=lam_init, scale=HEAD_DIM ** -0.5)
    out = pl.pallas_call(
        body,
        out_shape=jax.ShapeDtypeStruct((db, nrows, V_DIM), F32),
        grid_spec=pltpu.PrefetchScalarGridSpec(
            num_scalar_prefetch=1,
            grid=(db, n_pages // npg),
            in_specs=in_specs,
            out_specs=per_seq(nrows),
            scratch_shapes=[pltpu.VMEM((nrows, 1), F32), pltpu.VMEM((nrows, 1), F32),
                            pltpu.VMEM((nrows, V_DIM), F32)]),
        compiler_params=_params(("arbitrary", "arbitrary"), 48),
        name="attn_sample",
    )(page_table, *lam_vecs, subln_w.reshape(1, V_DIM), qbd, kn, vn, *([ck] * npg), *([cv] * npg))
    out = out.reshape(db, N_HEADS, 2, t_pad, V_DIM)[:, :, 0, :t]
    return out.transpose(0, 2, 1, 3).reshape(db * t, ATTN_W)


def _xattn_prompt_body(q_ref, k_ref, v_ref, o_ref, *, scale):
    s = _dot_nt(q_ref[...], k_ref[...]) * scale
    m = jnp.max(s, axis=-1, keepdims=True)
    p = jnp.exp(s - m)
    p = p / jnp.sum(p, axis=-1, keepdims=True)
    o_ref[...] = _dot(p.astype(BF16), v_ref[...]).astype(o_ref.dtype)


def _xattn_prompt(xq, mk, mv, b, s, n_mem):
    tq = _row_tile(s, 1024)
    nq = s // tq
    body = functools.partial(_xattn_prompt_body, scale=X_HEAD_DIM ** -0.5)
    return pl.pallas_call(
        body,
        out_shape=jax.ShapeDtypeStruct((b * s, X_W), BF16),
        grid=(b, X_HEADS, nq),
        in_specs=[pl.BlockSpec((tq, X_HEAD_DIM), lambda bi, h, i: (bi * nq + i, h)),
                  pl.BlockSpec((n_mem, X_HEAD_DIM), lambda bi, h, i: (bi, h)),
                  pl.BlockSpec((n_mem, X_HEAD_DIM), lambda bi, h, i: (bi, h))],
        out_specs=pl.BlockSpec((tq, X_HEAD_DIM), lambda bi, h, i: (bi * nq + i, h)),
        compiler_params=_params(("parallel", "parallel", "parallel"), 32),
        name="xattn_prompt",
    )(xq, mk, mv)


def _xattn_sample_body(q_ref, k_ref, v_ref, o_ref, *, t, n_mem, scale):
    g = k_ref.shape[0]
    rows = q_ref.shape[0]
    shape = (rows, g * n_mem)
    same = (lax.broadcasted_iota(jnp.int32, shape, 0) // t) == (lax.broadcasted_iota(jnp.int32, shape, 1) // n_mem)
    for h in range(X_HEADS):
        cols = slice(h * X_HEAD_DIM, (h + 1) * X_HEAD_DIM)
        kh = k_ref[:, :, cols].reshape(g * n_mem, X_HEAD_DIM).astype(BF16)
        vh = v_ref[:, :, cols].reshape(g * n_mem, X_HEAD_DIM).astype(BF16)
        s = jnp.where(same, _dot_nt(q_ref[:, cols], kh) * scale, NEG)
        m = jnp.max(s, axis=-1, keepdims=True)
        p = jnp.exp(s - m)
        p = p / jnp.sum(p, axis=-1, keepdims=True)
        o_ref[:, cols] = _dot(p.astype(BF16), vh).astype(o_ref.dtype)


def _xattn_sample(xq, mem_k, mem_v, db, t):
    n_mem = mem_k.shape[1]
    g = math.gcd(XATTN_GROUP, db)
    mk = mem_k.reshape(db, n_mem, X_W)
    mv = mem_v.reshape(db, n_mem, X_W)
    body = functools.partial(_xattn_sample_body, t=t, n_mem=n_mem, scale=X_HEAD_DIM ** -0.5)
    return pl.pallas_call(
        body,
        out_shape=jax.ShapeDtypeStruct((db * t, X_W), BF16),
        grid=(db // g,),
        in_specs=[pl.BlockSpec((g * t, X_W), lambda i: (i, 0)),
                  pl.BlockSpec((g, n_mem, X_W), lambda i: (i, 0, 0)),
                  pl.BlockSpec((g, n_mem, X_W), lambda i: (i, 0, 0))],
        out_specs=pl.BlockSpec((g * t, X_W), lambda i: (i, 0)),
        compiler_params=_params(("parallel",), 40),
        name="xattn_sample",
    )(xq, mk, mv)


POOL_HALO = 16


def _pool_prompt_body(x_ref, halo_ref, pw_ref, ps_ref, o_ref, buf, *, tm, tiles_per_seq):
    i = pl.program_id(0)
    first = (i % tiles_per_seq) == 0
    x = x_ref[...]
    buf[0:POOL_HALO, :] = jnp.where(first, 0.0, halo_ref[...])
    buf[POOL_HALO:, :] = x
    pos = (i % tiles_per_seq) * tm + lax.broadcasted_iota(jnp.int32, (tm, 1), 0)
    for g, w in enumerate(POOL_WINDOWS):
        cols = slice(g * POOL_GC, (g + 1) * POOL_GC)
        xs = x[:, cols]
        acc = xs
        for j in range(1, w):
            acc = acc + buf[POOL_HALO - j:POOL_HALO - j + tm, cols]
        cnt = jnp.minimum(w, pos + 1).astype(F32)
        d = acc / cnt - xs
        o_ref[:, cols] = (_dot(d.astype(BF16), pw_ref[g]) * ps_ref[:, cols]).astype(o_ref.dtype)


def _pool_prompt(pz, pool_w, pool_scale, s):
    m = pz.shape[0]
    tm = _row_tile(s, 512)
    hb = tm // POOL_HALO
    body = functools.partial(_pool_prompt_body, tm=tm, tiles_per_seq=s // tm)
    return pl.pallas_call(
        body,
        out_shape=jax.ShapeDtypeStruct((m, POOL_W), BF16),
        grid=(m // tm,),
        in_specs=[pl.BlockSpec((tm, POOL_W), lambda i: (i, 0)),
                  pl.BlockSpec((POOL_HALO, POOL_W), lambda i: (jnp.maximum(i * hb - 1, 0), 0)),
                  pl.BlockSpec(pool_w.shape, lambda i: (0, 0, 0)),
                  pl.BlockSpec((1, POOL_W), lambda i: (0, 0))],
        out_specs=pl.BlockSpec((tm, POOL_W), lambda i: (i, 0)),
        scratch_shapes=[pltpu.VMEM((tm + POOL_HALO, POOL_W), F32)],
        compiler_params=_params(("parallel",), 32),
        name="pool_prompt",
    )(pz, pz, pool_w, pool_scale.reshape(1, POOL_W))


def _pool_sample_body(st_ref, pz_ref, pw_ref, ps_ref, o_ref, *, n_state, t, past_len):
    def row(idx, cols):
        return st_ref[idx][:, cols] if idx < n_state else pz_ref[idx - n_state][:, cols]

    for g, w in enumerate(POOL_WINDOWS):
        cols = slice(g * POOL_GC, (g + 1) * POOL_GC)
        ds = []
        for ti in range(t):
            hi = n_state + ti + 1
            lo = max(hi - w, 0)
            acc = row(hi - 1, cols)
            for idx in range(hi - 2, lo - 1, -1):
                acc = acc + row(idx, cols)
            cnt = float(min(w, past_len + ti + 1))
            ds.append(acc / cnt - row(hi - 1, cols))
        d = jnp.concatenate(ds, axis=0)
        o_ref[:, cols] = (_dot(d.astype(BF16), pw_ref[g]) * ps_ref[:, cols]).astype(o_ref.dtype)


def _pool_sample(pz, state_pool, pool_w, pool_scale, db, t, past_len):
    n_state = state_pool.shape[1]
    st = state_pool.transpose(1, 0, 2)
    pzt = pz.reshape(db, t, POOL_W).transpose(1, 0, 2)
    body = functools.partial(_pool_sample_body, n_state=n_state, t=t, past_len=past_len)
    out = pl.pallas_call(
        body,
        out_shape=jax.ShapeDtypeStruct((t * db, POOL_W), BF16),
        grid=(1,),
        in_specs=[pl.BlockSpec(st.shape, lambda i: (0, 0, 0)),
                  pl.BlockSpec(pzt.shape, lambda i: (0, 0, 0)),
                  pl.BlockSpec(pool_w.shape, lambda i: (0, 0, 0)),
                  pl.BlockSpec((1, POOL_W), lambda i: (0, 0))],
        out_specs=pl.BlockSpec((t * db, POOL_W), lambda i: (0, 0)),
        compiler_params=_params(("arbitrary",), 40),
        name="pool_sample",
    )(st, pzt, pool_w, pool_scale.reshape(1, POOL_W))
    return out.reshape(t, db, POOL_W).transpose(1, 0, 2).reshape(db * t, POOL_W)


def _merge_body(a_ref, p_ref, c_ref, wa_ref, wp_ref, wc_ref, g0_ref, g1_ref, g2_ref, o_ref):
    m = (g0_ref[...] * _dot(a_ref[...], wa_ref[...])
         + g1_ref[...] * _dot(p_ref[...], wp_ref[...])
         + g2_ref[...] * _dot(c_ref[...], wc_ref[...]))
    o_ref[...] = m.astype(o_ref.dtype)


def _merge(a, po, c, gates, wa, wp, wc, d, tm):
    m = a.shape[0]
    assert m % tm == 0
    tn = 256
    nj = d // tn
    row = lambda k: pl.BlockSpec((tm, k), lambda i, j: (i, 0))
    wcol = lambda k: pl.BlockSpec((k, tn), lambda i, j: (0, j))
    gate = lambda r: pl.BlockSpec((tm, tn), lambda i, j: (i, r * nj + j))
    return pl.pallas_call(
        _merge_body,
        out_shape=jax.ShapeDtypeStruct((m, d), BF16),
        grid=(m // tm, nj),
        in_specs=[row(ATTN_W), row(POOL_W), row(X_W), wcol(ATTN_W), wcol(POOL_W), wcol(X_W),
                  gate(0), gate(1), gate(2)],
        out_specs=pl.BlockSpec((tm, tn), lambda i, j: (i, j)),
        compiler_params=_params(("parallel", "arbitrary"), 48),
        name="merge",
    )(a, po, c, wa, wp, wc, gates, gates, gates)


def _resid_body(x_ref, w_ref, r_ref, o_ref):
    o_ref[...] = r_ref[...] + _dot(x_ref[...], w_ref[...])


def _resid_proj(x, w, resid, name, *, tm, tn):
    m, k = x.shape
    n = w.shape[1]
    assert m % tm == 0 and n % tn == 0
    return pl.pallas_call(
        _resid_body,
        out_shape=jax.ShapeDtypeStruct((m, n), F32),
        grid=(m // tm, n // tn),
        in_specs=[pl.BlockSpec((tm, k), lambda i, j: (i, 0)),
                  pl.BlockSpec((k, tn), lambda i, j: (0, j)),
                  pl.BlockSpec((tm, tn), lambda i, j: (i, j))],
        out_specs=pl.BlockSpec((tm, tn), lambda i, j: (i, j)),
        compiler_params=_params(("parallel", "arbitrary"), 48),
        name=name,
    )(x, w, resid)


CONV_HALO = SUBLANES


def _ffn_a_body(*refs, tm, tiles_per_seq, seq_len, has_prev):
    h_ref, wg_ref, wu_ref, cw_ref, cb_ref = refs[:5]
    idx = 5
    if has_prev:
        p1_ref, p2_ref = refs[idx:idx + 2]
        idx += 2
    act_ref, g_ref, buf, carry = refs[idx:]
    i = pl.program_id(0)
    j = pl.program_id(1)
    h = h_ref[...]
    g = _dot(h, wg_ref[...])
    u = _dot(h, wu_ref[...])
    if tiles_per_seq:
        first = (i % tiles_per_seq) == 0
        buf[0:CONV_HALO, :] = jnp.where(first, 0.0, carry[j])
    else:
        buf[0:CONV_HALO, :] = jnp.zeros((CONV_HALO, g.shape[1]), F32)
    buf[CONV_HALO:, :] = g
    carry[j] = g[tm - CONV_HALO:, :]
    g1 = buf[CONV_HALO - 1:CONV_HALO - 1 + tm, :]
    g2 = buf[CONV_HALO - 2:CONV_HALO - 2 + tm, :]
    if not tiles_per_seq:
        t = lax.broadcasted_iota(jnp.int32, (tm, 1), 0) % seq_len
        g1 = jnp.where(t >= 1, g1, 0.0)
        g2 = jnp.where(t >= 2, g2, 0.0)
    if has_prev:
        g1 = g1 + p1_ref[...]
        g2 = g2 + p2_ref[...]
    cw = cw_ref[...]
    gc = cb_ref[...] + g2 * cw[0:1, :]
    gc = gc + g1 * cw[1:2, :]
    gc = gc + g * cw[2:3, :]
    act_ref[...] = (jax.nn.silu(gc) * u).astype(act_ref.dtype)
    if tiles_per_seq:
        g_ref[...] = g[tm - CONV_HALO:, :]
    else:
        g_ref[...] = g


def _ffn_a(h2, wg, wu, conv_w, conv_b, seq_len, tm, prev=None):
    m, d = h2.shape
    dff = wg.shape[1]
    tn = 256
    assert dff % tn == 0 and m % tm == 0
    long_seq = seq_len >= tm
    if long_seq:
        assert seq_len % tm == 0 and prev is None
        tiles_per_seq = seq_len // tm
    else:
        assert tm == m and tm % seq_len == 0
        tiles_per_seq = 0
    nj = dff // tn
    ins = [h2, wg, wu, conv_w, conv_b.reshape(1, dff)]
    in_specs = [pl.BlockSpec((tm, d), lambda i, j: (i, 0)),
                pl.BlockSpec((d, tn), lambda i, j: (0, j)),
                pl.BlockSpec((d, tn), lambda i, j: (0, j)),
                pl.BlockSpec((CONV_W, tn), lambda i, j: (0, j)),
                pl.BlockSpec((1, tn), lambda i, j: (0, j))]
    if prev is not None:
        ins += list(prev)
        in_specs += [pl.BlockSpec((tm, tn), lambda i, j: (i, j))] * 2
    if long_seq:
        g_shape = jax.ShapeDtypeStruct((m // tm, CONV_HALO, dff), F32)
        g_spec = pl.BlockSpec((None, CONV_HALO, tn), lambda i, j: (i, 0, j))
    else:
        g_shape = jax.ShapeDtypeStruct((m, dff), F32)
        g_spec = pl.BlockSpec((tm, tn), lambda i, j: (i, j))
    body = functools.partial(_ffn_a_body, tm=tm, tiles_per_seq=tiles_per_seq, seq_len=seq_len,
                             has_prev=prev is not None)
    return pl.pallas_call(
        body,
        out_shape=[jax.ShapeDtypeStruct((m, dff), BF16), g_shape],
        grid=(m // tm, nj),
        in_specs=in_specs,
        out_specs=[pl.BlockSpec((tm, tn), lambda i, j: (i, j)), g_spec],
        scratch_shapes=[pltpu.VMEM((tm + CONV_HALO, tn), F32), pltpu.VMEM((nj, CONV_HALO, tn), F32)],
        compiler_params=_params(("arbitrary", "arbitrary"), 48),
        name="ffn_gate_up",
    )(*ins)


def _rope_tables(pos):
    half = HEAD_DIM // 2
    inv = ROPE_THETA ** (-jnp.arange(half, dtype=F32) / half)
    ang = pos.astype(F32)[:, None] * inv[None, :]
    cos, sin = jnp.cos(ang), jnp.sin(ang)
    return jnp.concatenate([cos, cos], axis=-1), jnp.concatenate([-sin, sin], axis=-1)


def _layer(x2, w, lam_init, *, tm, seq_len, rope, attn_fn, pool_fn, xattn_fn, conv_prev, q_dtype):
    d = x2.shape[1]
    rope_rep = rope[0].shape[0] // tm
    pool_off = 3 * ATTN_W
    xq_off = pool_off + POOL_W
    gate_off = xq_off + X_W
    h = _rmsnorm(x2, w['attn_norm_w'], "attn_norm")
    win = w['w_in']
    (q,) = _proj(h, win, 0, ATTN_W, [q_dtype], "proj_q", tm=tm, norm_w=w['q_norm_w'], rope=rope,
                 rope_rep=rope_rep)
    k32, kbf = _proj(h, win, ATTN_W, ATTN_W, [F32, BF16], "proj_k", tm=tm, norm_w=w['k_norm_w'], rope=rope,
                     rope_rep=rope_rep)
    v32, vbf = _proj(h, win, 2 * ATTN_W, ATTN_W, [F32, BF16], "proj_v", tm=tm)
    (pz,) = _proj(h, win, pool_off, POOL_W, [F32], "proj_pool", tm=tm)
    (xq,) = _proj(h, win, xq_off, X_W, [BF16], "proj_xq", tm=tm, norm_w=w['xq_norm_w'])
    (gates,) = _proj(h, win, gate_off, 3 * d, [F32], "proj_gates", tm=tm, sigmoid=True)
    a = attn_fn(q, kbf, vbf)
    po = pool_fn(pz)
    c = xattn_fn(xq)
    m = _merge(a, po, c, gates, w['w_branch_attn'], w['w_branch_pool'], w['w_branch_cross'], d, tm)
    x1 = _resid_proj(m, w['w_out'], x2, "out_proj", tm=tm, tn=512)
    h2 = _rmsnorm(x1, w['ffn_norm_w'], "ffn_norm")
    act, gl = _ffn_a(h2, w['w_gate'], w['w_up'], w['conv_w'], w['conv_b'], seq_len, tm, conv_prev)
    y = _resid_proj(act, w['w_down'], x1, "down_proj", tm=min(tm, 512), tn=256)
    return y, k32, v32, pz, gl


def kernel(x_prompt, x_sample, cache_k, cache_v, cache_mem_k, cache_mem_v, state_pool, state_conv, page_table, mem_prompt, attn_norm_w, w_in, q_norm_w, k_norm_w, lambda_q1, lambda_k1, lambda_q2, lambda_k2, subln_w, pool_w, pool_scale, mem_norm_w, w_mem_k, w_mem_v, xq_norm_w, xk_norm_w, w_branch_attn, w_branch_pool, w_branch_cross, w_out, ffn_norm_w, w_gate, w_up, conv_w, conv_b, w_down):
    b, s, d = x_prompt.shape
    db, t, _ = x_sample.shape
    depth = w_in.shape[0]
    dff = w_gate.shape[-1]
    n_mem = mem_prompt.shape[1]
    past_len = page_table.shape[1] * cache_k.shape[2]
    n_state = state_pool.shape[2]
    rope_p = _rope_tables(jnp.arange(s, dtype=jnp.int32))
    rope_s = _rope_tables(jnp.tile(past_len + jnp.arange(t, dtype=jnp.int32), db))
    yp = x_prompt.reshape(b * s, d)
    ys = x_sample.reshape(db * t, d)
    outs = [[] for _ in range(10)]
    for l in range(depth):
        lam_init = 0.8 - 0.6 * math.exp(-0.3 * l)
        w = {'attn_norm_w': attn_norm_w[l], 'w_in': w_in[l].astype(BF16), 'q_norm_w': q_norm_w[l],
             'k_norm_w': k_norm_w[l], 'xq_norm_w': xq_norm_w[l],
             'w_branch_attn': w_branch_attn[l].astype(BF16), 'w_branch_pool': w_branch_pool[l].astype(BF16),
             'w_branch_cross': w_branch_cross[l].astype(BF16), 'w_out': w_out[l].astype(BF16),
             'ffn_norm_w': ffn_norm_w[l], 'w_gate': w_gate[l].astype(BF16), 'w_up': w_up[l].astype(BF16),
             'conv_w': conv_w[l], 'conv_b': conv_b[l], 'w_down': w_down[l].astype(BF16)}
        lam_vecs = [v[l].reshape(1, HEAD_DIM).astype(F32) for v in (lambda_q1, lambda_k1, lambda_q2, lambda_k2)]
        pw = pool_w[l].astype(BF16)
        ps = pool_scale[l]
        sw = subln_w[l]

        mn = _rmsnorm(mem_prompt.reshape(b * n_mem, d), mem_norm_w[l], "mem_norm")
        tm_m = _row_tile(b * n_mem, 1024)
        mk32, mkbf = _proj(mn, w_mem_k[l].astype(BF16), 0, X_W, [F32, BF16], "proj_mem_k", tm=tm_m,
                           norm_w=xk_norm_w[l])
        mv32, mvbf = _proj(mn, w_mem_v[l].astype(BF16), 0, X_W, [F32, BF16], "proj_mem_v", tm=tm_m)
        tm_p = _row_tile(s, 1024)
        yp, k_p, v_p, pz_p, gl_p = _layer(
            yp, w, lam_init, tm=tm_p, seq_len=s, rope=rope_p,
            attn_fn=lambda q, k, v: _attn_prompt(q, k, v, lam_vecs, sw, b, s, lam_init),
            pool_fn=lambda pz: _pool_prompt(pz, pw, ps, s),
            xattn_fn=lambda xq: _xattn_prompt(xq, mkbf, mvbf, b, s, n_mem),
            conv_prev=None, q_dtype=BF16)
        tiles_per_seq = s // tm_p
        conv_p = gl_p.reshape(b, tiles_per_seq, CONV_HALO, dff)[:, -1, CONV_HALO - (CONV_W - 1):]

        sc = state_conv[l]
        zrow = jnp.zeros((db, 1, dff), F32)
        p1 = jnp.concatenate([sc[:, 1:2]] + [zrow] * (t - 1), axis=1).reshape(db * t, dff)
        p2 = jnp.concatenate([sc[:, 0:1], sc[:, 1:2]] + [zrow] * (t - 2), axis=1).reshape(db * t, dff)
        cmk, cmv = cache_mem_k[l], cache_mem_v[l]
        sp = state_pool[l]
        ys, k_s, v_s, pz_s, g_s = _layer(
            ys, w, lam_init, tm=_row_tile(db * t, 1024), seq_len=t, rope=rope_s,
            attn_fn=lambda q, k, v: _attn_sample(q, k, v, cache_k, cache_v, l, page_table, lam_vecs, sw, db, t,
                                                 lam_init).astype(BF16),
            pool_fn=lambda pz: _pool_sample(pz, sp, pw, ps, db, t, past_len),
            xattn_fn=lambda xq: _xattn_sample(xq, cmk, cmv, db, t),
            conv_prev=(p1, p2), q_dtype=F32)
        pool_s = jnp.concatenate([sp, pz_s.reshape(db, t, POOL_W)], axis=1)[:, -n_state:]
        conv_s = jnp.concatenate([sc, g_s.reshape(db, t, dff)], axis=1)[:, -(CONV_W - 1):]

        vals = [k_p.reshape(b, s, N_HEADS, V_DIM), v_p.reshape(b, s, N_HEADS, V_DIM),
                mk32.reshape(b, n_mem, X_HEADS, X_HEAD_DIM), mv32.reshape(b, n_mem, X_HEADS, X_HEAD_DIM),
                pz_p.reshape(b, s, POOL_W)[:, -n_state:], conv_p,
                k_s.reshape(db, t, N_HEADS, V_DIM), v_s.reshape(db, t, N_HEADS, V_DIM), pool_s, conv_s]
        for o, v in zip(outs, vals):
            o.append(v)
    return (yp.reshape(b, s, d), ys.reshape(db, t, d)) + tuple(jnp.stack(o) for o in outs)
```

```python
import functools
import math

import jax
import jax.numpy as jnp
from jax import lax
from jax.experimental import pallas as pl
from jax.experimental.pallas import tpu as pltpu

F32 = jnp.float32
BF16 = jnp.bfloat16

N_HEADS = 8
HEAD_DIM = 128
V_DIM = 2 * HEAD_DIM
ATTN_W = N_HEADS * V_DIM
POOL_WINDOWS = (2, 4, 8, 16)
POOL_GC = 256
POOL_W = len(POOL_WINDOWS) * POOL_GC
X_HEADS = 4
X_HEAD_DIM = 256
X_W = X_HEADS * X_HEAD_DIM
CONV_W = 3
ROPE_THETA = 10000.0
EPS = 1e-6
NEG = -1e30
LOG2_E = math.log2(math.e)

SUBLANES = 8
BF16_ROWS = 16
LANES = 128
PAGES_PER_STEP = 8
XATTN_GROUP = 4
MIB = 2 ** 20
STREAM_VMEM_MIB = 56


def _params(sem, vmem_mib):
    return pltpu.CompilerParams(dimension_semantics=sem, vmem_limit_bytes=vmem_mib * MIB)


def _row_tile(m, cap):
    t = min(m, cap)
    assert m % t == 0, (m, t)
    return t


def _dot(a, b):
    return jnp.dot(a, b, preferred_element_type=F32)


def _dot_nt(a, b):
    return lax.dot_general(a, b, (((1,), (1,)), ((), ())), preferred_element_type=F32)


def _rmsnorm_body(x_ref, w_ref, o_ref):
    x = x_ref[...]
    ms = jnp.mean(x * x, axis=-1, keepdims=True)
    o_ref[...] = (x * lax.rsqrt(ms + EPS) * w_ref[...]).astype(o_ref.dtype)


def _rmsnorm(x, w, name):
    m, d = x.shape
    tm = _row_tile(m, 256)
    return pl.pallas_call(
        _rmsnorm_body,
        out_shape=jax.ShapeDtypeStruct((m, d), BF16),
        grid=(m // tm,),
        in_specs=[pl.BlockSpec((tm, d), lambda i: (i, 0)), pl.BlockSpec((1, d), lambda i: (0, 0))],
        out_specs=pl.BlockSpec((tm, d), lambda i: (i, 0)),
        compiler_params=_params(("parallel",), 32),
        name=name,
    )(x, w.reshape(1, d))


def _step_maps(ni, total):
    def mm(s):
        sm = jnp.minimum(s, total - 1)
        return sm % ni, sm // ni

    def ep(s):
        sp = jnp.maximum(s - 1, 0)
        return sp % ni, sp // ni

    return mm, ep


def _ping_pong(s, step, bufs_a, bufs_b):
    @pl.when(s % 2 == 0)
    def _():
        step(bufs_a, bufs_b)

    @pl.when(s % 2 == 1)
    def _():
        step(bufs_b, bufs_a)


def _proj_body(*refs, ni, tn, chunk, rotary, sigmoid, n_out, delayed):
    x_ref, w_ref = refs[0], refs[1]
    idx = 2
    nw_ref = cos_ref = sin_ref = None
    if chunk:
        nw_ref = refs[idx]
        idx += 1
    if rotary:
        cos_ref, sin_ref = refs[idx], refs[idx + 1]
        idx += 2
    outs = refs[idx:idx + n_out]
    scratch = refs[idx + n_out:]
    wbf = scratch[0]
    s = pl.program_id(0)

    @pl.when(s % ni == 0)
    def _():
        wbf[...] = w_ref[...].astype(BF16)

    def epilogue(z):
        if sigmoid:
            z = jax.nn.sigmoid(z)
        if not chunk:
            for o in outs:
                o[...] = z.astype(o.dtype)
            return
        nw = nw_ref[...]
        for c in range(tn // chunk):
            cols = slice(c * chunk, (c + 1) * chunk)
            zc = z[:, cols]
            ms = jnp.mean(zc * zc, axis=-1, keepdims=True)
            y = zc * lax.rsqrt(ms + EPS) * nw
            if rotary:
                y = y * cos_ref[...] + pltpu.roll(y, chunk // 2, 1) * sin_ref[...]
            for o in outs:
                o[:, cols] = y.astype(o.dtype)

    if not delayed:
        epilogue(_dot(x_ref[...], wbf[...]))
        return

    za, zb = scratch[1:]

    @pl.when(s == 0)
    def _():
        zb[...] = jnp.zeros(zb.shape, F32)

    def step(z_w, z_r):
        z_w[0][...] = _dot(x_ref[...], wbf[...])
        epilogue(z_r[0][...])

    _ping_pong(s, step, (za,), (zb,))


def _proj(x, w, col0, ncols, out_dtypes, name, *, tm, norm_w=None, rope=None, rope_rep=1, sigmoid=False,
          tn=512):
    m, k = x.shape
    assert m % tm == 0 and ncols % tn == 0 and col0 % tn == 0
    cb0 = col0 // tn
    ni, nj = m // tm, ncols // tn
    total = ni * nj
    chunk = 0 if norm_w is None else norm_w.shape[-1]
    delayed = bool(chunk)
    if delayed:
        mm, ep = _step_maps(ni, total)
    else:
        mm = ep = lambda s: (s % ni, s // ni)
    ins = [x, w]
    in_specs = [pl.BlockSpec((tm, k), lambda s: (mm(s)[0], 0)),
                pl.BlockSpec((k, tn), lambda s: (0, cb0 + mm(s)[1]))]
    if chunk:
        ins.append(norm_w.reshape(1, chunk).astype(F32))
        in_specs.append(pl.BlockSpec((1, chunk), lambda s: (0, 0)))
    if rope is not None:
        for t in rope:
            ins.append(t)
            in_specs.append(pl.BlockSpec((tm, chunk), lambda s: (ep(s)[0] % rope_rep, 0)))
    out_shape = [jax.ShapeDtypeStruct((m, ncols), dt) for dt in out_dtypes]
    out_specs = [pl.BlockSpec((tm, tn), lambda s: ep(s)) for _ in out_dtypes]
    body = functools.partial(_proj_body, ni=ni, tn=tn, chunk=chunk, rotary=rope is not None, sigmoid=sigmoid,
                             n_out=len(out_dtypes), delayed=delayed)
    scratch = [pltpu.VMEM((k, tn), BF16)]
    if delayed:
        scratch += [pltpu.VMEM((tm, tn), F32), pltpu.VMEM((tm, tn), F32)]
    res = pl.pallas_call(
        body,
        out_shape=out_shape,
        grid=(total + int(delayed),),
        in_specs=in_specs,
        out_specs=out_specs,
        scratch_shapes=scratch,
        compiler_params=_params(("arbitrary",), STREAM_VMEM_MIB),
        name=name,
    )(*ins)
    return res


def _lam_value(lq1, lk1, lq2, lk2, lam_init):
    a = jnp.sum(lq1[...] * lk1[...], axis=-1, keepdims=True)
    b = jnp.sum(lq2[...] * lk2[...], axis=-1, keepdims=True)
    return jnp.exp(a) - jnp.exp(b) + lam_init


def _subln(o, sw, lam_init):
    ms = jnp.mean(o * o, axis=-1, keepdims=True)
    return (o * lax.rsqrt(ms + EPS) * sw) * (1.0 - lam_init)


def _attn_body(lq1, lk1, lq2, lk2, sw_ref, q_ref, k_ref, v_ref, o_ref,
               m1, l1, a1, m2, l2, a2, *, tq, lam_init, scale):
    qi = pl.program_id(2)
    q = q_ref[...]
    qs = (q[:, :HEAD_DIM], q[:, HEAD_DIM:])
    stats = ((m1, l1, a1), (m2, l2, a2))
    for m, l, a in stats:
        m[...] = jnp.full(m.shape, NEG, F32)
        l[...] = jnp.zeros(l.shape, F32)
        a[...] = jnp.zeros(a.shape, F32)

    def scores(j):
        start = pl.multiple_of(j * tq, tq)
        kb = k_ref[pl.ds(start, tq), :]
        return tuple(_dot_nt(qs[c], kb[:, c * HEAD_DIM:(c + 1) * HEAD_DIM]) for c in range(2))

    def accumulate(j, ss, mask):
        start = pl.multiple_of(j * tq, tq)
        vb = v_ref[pl.ds(start, tq), :]
        for s, (m, l, a) in zip(ss, stats):
            if mask is not None:
                s = jnp.where(mask, s, NEG)
            m_old = m[...]
            m_new = jnp.maximum(m_old, jnp.max(s, axis=-1, keepdims=True))
            alpha = jnp.exp2((m_old - m_new) * (scale * LOG2_E))
            p = jnp.exp2((s - m_new) * (scale * LOG2_E))
            l[...] = alpha * l[...] + jnp.sum(p, axis=-1, keepdims=True)
            a[...] = alpha * a[...] + _dot(p.astype(BF16), vb)
            m[...] = m_new

    def body(j, ss):
        nxt = scores(j + 1)
        accumulate(j, ss, None)
        return nxt

    ss = lax.fori_loop(0, qi, body, scores(0))
    row = lax.broadcasted_iota(jnp.int32, (tq, tq), 0)
    col = lax.broadcasted_iota(jnp.int32, (tq, tq), 1)
    accumulate(qi, ss, col <= row)

    lam = _lam_value(lq1, lk1, lq2, lk2, lam_init)
    o = a1[...] / l1[...] - lam * (a2[...] / l2[...])
    o_ref[...] = _subln(o, sw_ref[...], lam_init).astype(o_ref.dtype)


def _attn_prompt(q, k, v, lam_vecs, subln_w, b, s, lam_init):
    tq = _row_tile(s, 512)
    nq = s // tq
    vec = pl.BlockSpec((1, HEAD_DIM), lambda bi, h, i: (0, 0))
    body = functools.partial(_attn_body, tq=tq, lam_init=lam_init, scale=HEAD_DIM ** -0.5)
    return pl.pallas_call(
        body,
        out_shape=jax.ShapeDtypeStruct((b * s, ATTN_W), BF16),
        grid=(b, N_HEADS, nq),
        in_specs=[vec, vec, vec, vec,
                  pl.BlockSpec((1, V_DIM), lambda bi, h, i: (0, 0)),
                  pl.BlockSpec((tq, V_DIM), lambda bi, h, i: (bi * nq + i, h)),
                  pl.BlockSpec((s, V_DIM), lambda bi, h, i: (bi, h)),
                  pl.BlockSpec((s, V_DIM), lambda bi, h, i: (bi, h))],
        out_specs=pl.BlockSpec((tq, V_DIM), lambda bi, h, i: (bi * nq + i, h)),
        scratch_shapes=[pltpu.VMEM((tq, 1), F32), pltpu.VMEM((tq, 1), F32), pltpu.VMEM((tq, V_DIM), F32),
                        pltpu.VMEM((tq, 1), F32), pltpu.VMEM((tq, 1), F32), pltpu.VMEM((tq, V_DIM), F32)],
        compiler_params=_params(("parallel", "parallel", "arbitrary"), 40),
        name="attn_prompt",
    )(*lam_vecs, subln_w.reshape(1, V_DIM), q, k, v)


def _sattn_body(pt_ref, lq1, lk1, lq2, lk2, sw_ref, q_ref, kn_ref, vn_ref, *rest, npg, t_pad, lam_init, scale):
    del pt_ref
    k_refs = rest[:npg]
    v_refs = rest[npg:2 * npg]
    o_ref = rest[2 * npg]
    m_ref, l_ref, acc_ref = rest[2 * npg + 1:]
    c = pl.program_id(1)
    q = q_ref[...].astype(BF16)
    nrows = q.shape[0]

    def masks(ncols):
        row = lax.broadcasted_iota(jnp.int32, (nrows, ncols), 0)
        col = lax.broadcasted_iota(jnp.int32, (nrows, ncols), 1)
        same_head = (row // (2 * t_pad)) == (col % N_HEADS)
        return same_head, row % t_pad, col // N_HEADS

    def update(kbs, vbs, mask):
        ss = [jnp.where(mask, _dot_nt(q, kb) * scale, NEG) for kb in kbs]
        m_cur = functools.reduce(jnp.maximum, [jnp.max(s, axis=-1, keepdims=True) for s in ss])
        m_old = m_ref[...]
        m_new = jnp.maximum(m_old, m_cur)
        alpha = jnp.exp(m_old - m_new)
        ps = [jnp.exp(s - m_new) for s in ss]
        l_ref[...] = alpha * l_ref[...] + sum(jnp.sum(p, axis=-1, keepdims=True) for p in ps)
        acc_ref[...] = alpha * acc_ref[...] + sum(_dot(p.astype(BF16), vb) for p, vb in zip(ps, vbs))
        m_ref[...] = m_new

    @pl.when(c == 0)
    def _():
        m_ref[...] = jnp.full(m_ref.shape, NEG, F32)
        l_ref[...] = jnp.zeros(l_ref.shape, F32)
        acc_ref[...] = jnp.zeros(acc_ref.shape, F32)
        same_head, step, tok = masks(kn_ref.shape[0])
        update([kn_ref[...]], [vn_ref[...]], same_head & (tok <= step))

    update([r[...].astype(BF16) for r in k_refs], [r[...].astype(BF16) for r in v_refs],
           masks(k_refs[0].shape[0])[0])

    @pl.when(c == pl.num_programs(1) - 1)
    def _():
        lam = _lam_value(lq1, lk1, lq2, lk2, lam_init)
        y = acc_ref[...] / l_ref[...]
        o = y - lam * pltpu.roll(y, nrows - t_pad, 0)
        o_ref[...] = _subln(o, sw_ref[...], lam_init)


def _attn_sample(q32, k_bf, v_bf, cache_k, cache_v, layer, page_table, lam_vecs, subln_w, db, t, lam_init):
    t_pad = SUBLANES // 2
    assert t <= t_pad
    depth, n_phys, page, _, _ = cache_k.shape
    n_pages = page_table.shape[1]
    npg = math.gcd(PAGES_PER_STEP, n_pages)
    ck = cache_k.reshape(depth, n_phys, page * N_HEADS, V_DIM)
    cv = cache_v.reshape(depth, n_phys, page * N_HEADS, V_DIM)
    q5 = q32.reshape(db, t, N_HEADS, 2, HEAD_DIM).transpose(0, 2, 3, 1, 4)
    q5 = jnp.pad(q5, ((0, 0), (0, 0), (0, 0), (0, t_pad - t), (0, 0)))
    eye = jnp.eye(2, dtype=F32)
    nrows = N_HEADS * 2 * t_pad
    qbd = (q5[:, :, :, :, None, :] * eye[None, None, :, None, :, None]).reshape(db, nrows, V_DIM)
    new_tok = LANES // N_HEADS
    pad = ((0, 0), (0, new_tok - t), (0, 0), (0, 0))
    kn = jnp.pad(k_bf.reshape(db, t, N_HEADS, V_DIM), pad).reshape(db, LANES, V_DIM)
    vn = jnp.pad(v_bf.reshape(db, t, N_HEADS, V_DIM), pad).reshape(db, LANES, V_DIM)

    def page_spec(i):
        return pl.BlockSpec((None, None, page * N_HEADS, V_DIM),
                            lambda b, c, pt: (layer, pt[b, c * npg + i], 0, 0))

    vec = pl.BlockSpec((1, HEAD_DIM), lambda b, c, pt: (0, 0))
    per_seq = lambda rows: pl.BlockSpec((None, rows, V_DIM), lambda b, c, pt: (b, 0, 0))
    in_specs = [vec, vec, vec, vec, pl.BlockSpec((1, V_DIM), lambda b, c, pt: (0, 0)),
                per_seq(nrows), per_seq(LANES), per_seq(LANES)]
    in_specs += [page_spec(i) for i in range(npg)] * 2
    body = functools.partial(_sattn_body, npg=npg, t_pad=t_pad, lam_init=lam_init, scale=HEAD_DIM ** -0.5)
    out = pl.pallas_call(
        body,
        out_shape=jax.ShapeDtypeStruct((db, nrows, V_DIM), F32),
        grid_spec=pltpu.PrefetchScalarGridSpec(
            num_scalar_prefetch=1,
            grid=(db, n_pages // npg),
            in_specs=in_specs,
            out_specs=per_seq(nrows),
            scratch_shapes=[pltpu.VMEM((nrows, 1), F32), pltpu.VMEM((nrows, 1), F32),
                            pltpu.VMEM((nrows, V_DIM), F32)]),
        compiler_params=_params(("arbitrary", "arbitrary"), 48),
        name="attn_sample",
    )(page_table, *lam_vecs, subln_w.reshape(1, V_DIM), qbd, kn, vn, *([ck] * npg), *([cv] * npg))
    out = out.reshape(db, N_HEADS, 2, t_pad, V_DIM)[:, :, 0, :t]
    return out.transpose(0, 2, 1, 3).reshape(db * t, ATTN_W)


def _xattn_prompt_body(q_ref, k_ref, v_ref, o_ref, *, scale):
    s = _dot_nt(q_ref[...], k_ref[...]) * scale
    m = jnp.max(s, axis=-1, keepdims=True)
    p = jnp.exp(s - m)
    p = p / jnp.sum(p, axis=-1, keepdims=True)
    o_ref[...] = _dot(p.astype(BF16), v_ref[...]).astype(o_ref.dtype)


def _xattn_prompt(xq, mk, mv, b, s, n_mem):
    tq = _row_tile(s, 1024)
    nq = s // tq
    body = functools.partial(_xattn_prompt_body, scale=X_HEAD_DIM ** -0.5)
    return pl.pallas_call(
        body,
        out_shape=jax.ShapeDtypeStruct((b * s, X_W), BF16),
        grid=(b, X_HEADS, nq),
        in_specs=[pl.BlockSpec((tq, X_HEAD_DIM), lambda bi, h, i: (bi * nq + i, h)),
                  pl.BlockSpec((n_mem, X_HEAD_DIM), lambda bi, h, i: (bi, h)),
                  pl.BlockSpec((n_mem, X_HEAD_DIM), lambda bi, h, i: (bi, h))],
        out_specs=pl.BlockSpec((tq, X_HEAD_DIM), lambda bi, h, i: (bi * nq + i, h)),
        compiler_params=_params(("parallel", "parallel", "parallel"), 32),
        name="xattn_prompt",
    )(xq, mk, mv)


def _xattn_sample_body(q_ref, k_ref, v_ref, o_ref, *, t, scale):
    g, n_mem = k_ref.shape[0], k_ref.shape[1]
    rows = q_ref.shape[0]
    ncols = g * n_mem * X_HEADS
    kb = k_ref[...].reshape(ncols, X_HEAD_DIM).astype(BF16)
    vb = v_ref[...].reshape(ncols, X_HEAD_DIM).astype(BF16)
    row = lax.broadcasted_iota(jnp.int32, (rows, ncols), 0)
    col = lax.broadcasted_iota(jnp.int32, (rows, ncols), 1)
    same = ((row // (t * X_HEADS)) == (col // (n_mem * X_HEADS))) & ((row % X_HEADS) == (col % X_HEADS))
    s = jnp.where(same, _dot_nt(q_ref[...], kb) * scale, NEG)
    m = jnp.max(s, axis=-1, keepdims=True)
    p = jnp.exp(s - m)
    p = p / jnp.sum(p, axis=-1, keepdims=True)
    o_ref[...] = _dot(p.astype(BF16), vb).astype(o_ref.dtype)


def _xattn_sample(xq, mem_k, mem_v, layer, db, t):
    n_mem = mem_k.shape[2]
    g = math.gcd(XATTN_GROUP, db)
    rows = g * t * X_HEADS
    q = xq.reshape(db * t * X_HEADS, X_HEAD_DIM)
    mem_spec = pl.BlockSpec((None, g, n_mem, X_HEADS, X_HEAD_DIM), lambda i: (layer, i, 0, 0, 0))
    body = functools.partial(_xattn_sample_body, t=t, scale=X_HEAD_DIM ** -0.5)
    out = pl.pallas_call(
        body,
        out_shape=jax.ShapeDtypeStruct((db * t * X_HEADS, X_HEAD_DIM), BF16),
        grid=(db // g,),
        in_specs=[pl.BlockSpec((rows, X_HEAD_DIM), lambda i: (i, 0)), mem_spec, mem_spec],
        out_specs=pl.BlockSpec((rows, X_HEAD_DIM), lambda i: (i, 0)),
        compiler_params=_params(("parallel",), 40),
        name="xattn_sample",
    )(q, mem_k, mem_v)
    return out.reshape(db * t, X_W)


POOL_HALO = 16


def _pool_prompt_body(x_ref, halo_ref, pw_ref, ps_ref, o_ref, buf, *, tm, tiles_per_seq):
    i = pl.program_id(0)
    first = (i % tiles_per_seq) == 0
    x = x_ref[...]
    buf[0:POOL_HALO, :] = jnp.where(first, 0.0, halo_ref[...])
    buf[POOL_HALO:, :] = x
    pos = (i % tiles_per_seq) * tm + lax.broadcasted_iota(jnp.int32, (tm, 1), 0)
    for g, w in enumerate(POOL_WINDOWS):
        cols = slice(g * POOL_GC, (g + 1) * POOL_GC)
        xs = x[:, cols]
        acc = xs
        for j in range(1, w):
            acc = acc + buf[POOL_HALO - j:POOL_HALO - j + tm, cols]
        cnt = jnp.minimum(w, pos + 1).astype(F32)
        d = acc / cnt - xs
        o_ref[:, cols] = (_dot(d.astype(BF16), pw_ref[g]) * ps_ref[:, cols]).astype(o_ref.dtype)


def _pool_prompt(pz, pool_w, pool_scale, s):
    m = pz.shape[0]
    tm = _row_tile(s, 512)
    hb = tm // POOL_HALO
    body = functools.partial(_pool_prompt_body, tm=tm, tiles_per_seq=s // tm)
    return pl.pallas_call(
        body,
        out_shape=jax.ShapeDtypeStruct((m, POOL_W), BF16),
        grid=(m // tm,),
        in_specs=[pl.BlockSpec((tm, POOL_W), lambda i: (i, 0)),
                  pl.BlockSpec((POOL_HALO, POOL_W), lambda i: (jnp.maximum(i * hb - 1, 0), 0)),
                  pl.BlockSpec(pool_w.shape, lambda i: (0, 0, 0)),
                  pl.BlockSpec((1, POOL_W), lambda i: (0, 0))],
        out_specs=pl.BlockSpec((tm, POOL_W), lambda i: (i, 0)),
        scratch_shapes=[pltpu.VMEM((tm + POOL_HALO, POOL_W), F32)],
        compiler_params=_params(("parallel",), 32),
        name="pool_prompt",
    )(pz, pz, pool_w, pool_scale.reshape(1, POOL_W))


def _pool_sample_body(st_ref, pz_ref, pw_ref, ps_ref, o_ref, *, n_state, t, past_len):
    def row(idx, cols):
        return st_ref[idx][:, cols] if idx < n_state else pz_ref[idx - n_state][:, cols]

    for g, w in enumerate(POOL_WINDOWS):
        cols = slice(g * POOL_GC, (g + 1) * POOL_GC)
        ds = []
        for ti in range(t):
            hi = n_state + ti + 1
            lo = max(hi - w, 0)
            acc = row(hi - 1, cols)
            for idx in range(hi - 2, lo - 1, -1):
                acc = acc + row(idx, cols)
            cnt = float(min(w, past_len + ti + 1))
            ds.append(acc / cnt - row(hi - 1, cols))
        d = jnp.concatenate(ds, axis=0)
        o_ref[:, cols] = (_dot(d.astype(BF16), pw_ref[g]) * ps_ref[:, cols]).astype(o_ref.dtype)


def _pool_sample(pz, state_pool, pool_w, pool_scale, db, t, past_len):
    n_state = state_pool.shape[1]
    st = state_pool.transpose(1, 0, 2)
    pzt = pz.reshape(db, t, POOL_W).transpose(1, 0, 2)
    body = functools.partial(_pool_sample_body, n_state=n_state, t=t, past_len=past_len)
    out = pl.pallas_call(
        body,
        out_shape=jax.ShapeDtypeStruct((t * db, POOL_W), BF16),
        grid=(1,),
        in_specs=[pl.BlockSpec(st.shape, lambda i: (0, 0, 0)),
                  pl.BlockSpec(pzt.shape, lambda i: (0, 0, 0)),
                  pl.BlockSpec(pool_w.shape, lambda i: (0, 0, 0)),
                  pl.BlockSpec((1, POOL_W), lambda i: (0, 0))],
        out_specs=pl.BlockSpec((t * db, POOL_W), lambda i: (0, 0)),
        compiler_params=_params(("arbitrary",), 40),
        name="pool_sample",
    )(st, pzt, pool_w, pool_scale.reshape(1, POOL_W))
    return out.reshape(t, db, POOL_W).transpose(1, 0, 2).reshape(db * t, POOL_W)


def _merge_body(a_ref, p_ref, c_ref, wa_ref, wp_ref, wc_ref, g0_ref, g1_ref, g2_ref, o_ref):
    m = (g0_ref[...] * _dot(a_ref[...], wa_ref[...])
         + g1_ref[...] * _dot(p_ref[...], wp_ref[...])
         + g2_ref[...] * _dot(c_ref[...], wc_ref[...]))
    o_ref[...] = m.astype(o_ref.dtype)


def _merge(a, po, c, gates, wa, wp, wc, d, tm):
    m = a.shape[0]
    assert m % tm == 0
    tn = 256
    nj = d // tn
    row = lambda k: pl.BlockSpec((tm, k), lambda i, j: (i, 0))
    wcol = lambda k: pl.BlockSpec((k, tn), lambda i, j: (0, j))
    gate = lambda r: pl.BlockSpec((tm, tn), lambda i, j: (i, r * nj + j))
    return pl.pallas_call(
        _merge_body,
        out_shape=jax.ShapeDtypeStruct((m, d), BF16),
        grid=(m // tm, nj),
        in_specs=[row(ATTN_W), row(POOL_W), row(X_W), wcol(ATTN_W), wcol(POOL_W), wcol(X_W),
                  gate(0), gate(1), gate(2)],
        out_specs=pl.BlockSpec((tm, tn), lambda i, j: (i, j)),
        compiler_params=_params(("parallel", "arbitrary"), 48),
        name="merge",
    )(a, po, c, wa, wp, wc, gates, gates, gates)


def _resid_body(x_ref, w_ref, r_ref, o_ref):
    o_ref[...] = r_ref[...] + _dot(x_ref[...], w_ref[...])


def _resid_proj(x, w, resid, name, *, tm, tn):
    m, k = x.shape
    n = w.shape[1]
    assert m % tm == 0 and n % tn == 0
    return pl.pallas_call(
        _resid_body,
        out_shape=jax.ShapeDtypeStruct((m, n), F32),
        grid=(m // tm, n // tn),
        in_specs=[pl.BlockSpec((tm, k), lambda i, j: (i, 0)),
                  pl.BlockSpec((k, tn), lambda i, j: (0, j)),
                  pl.BlockSpec((tm, tn), lambda i, j: (i, j))],
        out_specs=pl.BlockSpec((tm, tn), lambda i, j: (i, j)),
        compiler_params=_params(("parallel", "arbitrary"), 48),
        name=name,
    )(x, w, resid)


CONV_HALO = SUBLANES


def _ffn_a_body(*refs, ni, tm, tiles_per_seq, seq_len, has_prev):
    h_ref, wg_ref, wu_ref, cw_ref, cb_ref = refs[:5]
    idx = 5
    if has_prev:
        p1_ref, p2_ref = refs[idx:idx + 2]
        idx += 2
    act_ref, g_ref, wgb, wub, buf, carry = refs[idx:]
    s = pl.program_id(0)

    @pl.when(s == 0)
    def _():
        carry[...] = jnp.zeros(carry.shape, F32)

    @pl.when(s % ni == 0)
    def _():
        wgb[...] = wg_ref[...].astype(BF16)
        wub[...] = wu_ref[...].astype(BF16)

    h = h_ref[...]
    g = _dot(h, wgb[...])
    u = _dot(h, wub[...])
    if tiles_per_seq:
        first = ((s % ni) % tiles_per_seq) == 0
        buf[0:CONV_HALO, :] = jnp.where(first, 0.0, carry[...])
    else:
        buf[0:CONV_HALO, :] = jnp.zeros((CONV_HALO, g.shape[1]), F32)
    buf[CONV_HALO:, :] = g
    carry[...] = g[tm - CONV_HALO:, :]
    g1 = buf[CONV_HALO - 1:CONV_HALO - 1 + tm, :]
    g2 = buf[CONV_HALO - 2:CONV_HALO - 2 + tm, :]
    if not tiles_per_seq:
        t = lax.broadcasted_iota(jnp.int32, (tm, 1), 0) % seq_len
        g1 = jnp.where(t >= 1, g1, 0.0)
        g2 = jnp.where(t >= 2, g2, 0.0)
    if has_prev:
        g1 = g1 + p1_ref[...]
        g2 = g2 + p2_ref[...]
    cw = cw_ref[...]
    gc = cb_ref[...] + g2 * cw[0:1, :]
    gc = gc + g1 * cw[1:2, :]
    gc = gc + g * cw[2:3, :]
    act_ref[...] = (jax.nn.silu(gc) * u).astype(act_ref.dtype)
    if tiles_per_seq:
        g_ref[...] = g[tm - CONV_HALO:, :]
    else:
        g_ref[...] = g


def _ffn_a(h2, wg, wu, conv_w, conv_b, seq_len, tm, prev=None):
    m, d = h2.shape
    dff = wg.shape[1]
    tn = 256
    assert dff % tn == 0 and m % tm == 0
    long_seq = seq_len >= tm
    if long_seq:
        assert seq_len % tm == 0 and prev is None
        tiles_per_seq = seq_len // tm
    else:
        assert tm == m and tm % seq_len == 0
        tiles_per_seq = 0
    ni, nj = m // tm, dff // tn
    tile = lambda s: (s % ni, s // ni)
    col = lambda s: (0, s // ni)
    ins = [h2, wg, wu, conv_w, conv_b.reshape(1, dff)]
    in_specs = [pl.BlockSpec((tm, d), lambda s: (s % ni, 0)),
                pl.BlockSpec((d, tn), col),
                pl.BlockSpec((d, tn), col),
                pl.BlockSpec((CONV_W, tn), col),
                pl.BlockSpec((1, tn), col)]
    if prev is not None:
        ins += list(prev)
        in_specs += [pl.BlockSpec((tm, tn), tile)] * 2
    if long_seq:
        g_shape = jax.ShapeDtypeStruct((ni, CONV_HALO, dff), F32)
        g_spec = pl.BlockSpec((None, CONV_HALO, tn), lambda s: (s % ni, 0, s // ni))
    else:
        g_shape = jax.ShapeDtypeStruct((m, dff), F32)
        g_spec = pl.BlockSpec((tm, tn), tile)
    body = functools.partial(_ffn_a_body, ni=ni, tm=tm, tiles_per_seq=tiles_per_seq, seq_len=seq_len,
                             has_prev=prev is not None)
    return pl.pallas_call(
        body,
        out_shape=[jax.ShapeDtypeStruct((m, dff), BF16), g_shape],
        grid=(ni * nj,),
        in_specs=in_specs,
        out_specs=[pl.BlockSpec((tm, tn), tile), g_spec],
        scratch_shapes=[pltpu.VMEM((d, tn), BF16), pltpu.VMEM((d, tn), BF16),
                        pltpu.VMEM((tm + CONV_HALO, tn), F32), pltpu.VMEM((CONV_HALO, tn), F32)],
        compiler_params=_params(("arbitrary",), STREAM_VMEM_MIB),
        name="ffn_gate_up",
    )(*ins)


def _rope_tables(pos):
    half = HEAD_DIM // 2
    inv = ROPE_THETA ** (-jnp.arange(half, dtype=F32) / half)
    ang = pos.astype(F32)[:, None] * inv[None, :]
    cos, sin = jnp.cos(ang), jnp.sin(ang)
    return jnp.concatenate([cos, cos], axis=-1), jnp.concatenate([-sin, sin], axis=-1)


def _layer(x2, w, lam_init, *, tm, seq_len, rope, attn_fn, pool_fn, xattn_fn, conv_prev, q_dtype):
    d = x2.shape[1]
    rope_rep = rope[0].shape[0] // tm
    pool_off = 3 * ATTN_W
    xq_off = pool_off + POOL_W
    gate_off = xq_off + X_W
    h = _rmsnorm(x2, w['attn_norm_w'], "attn_norm")
    win = w['w_in']
    (q,) = _proj(h, win, 0, ATTN_W, [q_dtype], "proj_q", tm=tm, norm_w=w['q_norm_w'], rope=rope,
                 rope_rep=rope_rep)
    k32, kbf = _proj(h, win, ATTN_W, ATTN_W, [F32, BF16], "proj_k", tm=tm, norm_w=w['k_norm_w'], rope=rope,
                     rope_rep=rope_rep)
    v32, vbf = _proj(h, win, 2 * ATTN_W, ATTN_W, [F32, BF16], "proj_v", tm=tm)
    (pz,) = _proj(h, win, pool_off, POOL_W, [F32], "proj_pool", tm=tm)
    (xq,) = _proj(h, win, xq_off, X_W, [BF16], "proj_xq", tm=tm, norm_w=w['xq_norm_w'])
    (gates,) = _proj(h, win, gate_off, 3 * d, [F32], "proj_gates", tm=tm, sigmoid=True)
    a = attn_fn(q, kbf, vbf)
    po = pool_fn(pz)
    c = xattn_fn(xq)
    m = _merge(a, po, c, gates, w['w_branch_attn'], w['w_branch_pool'], w['w_branch_cross'], d, tm)
    x1 = _resid_proj(m, w['w_out'], x2, "out_proj", tm=tm, tn=512)
    h2 = _rmsnorm(x1, w['ffn_norm_w'], "ffn_norm")
    act, gl = _ffn_a(h2, w['w_gate'], w['w_up'], w['conv_w'], w['conv_b'], seq_len, tm, conv_prev)
    y = _resid_proj(act, w['w_down'], x1, "down_proj", tm=min(tm, 512), tn=256)
    return y, k32, v32, pz, gl


def kernel(x_prompt, x_sample, cache_k, cache_v, cache_mem_k, cache_mem_v, state_pool, state_conv, page_table, mem_prompt, attn_norm_w, w_in, q_norm_w, k_norm_w, lambda_q1, lambda_k1, lambda_q2, lambda_k2, subln_w, pool_w, pool_scale, mem_norm_w, w_mem_k, w_mem_v, xq_norm_w, xk_norm_w, w_branch_attn, w_branch_pool, w_branch_cross, w_out, ffn_norm_w, w_gate, w_up, conv_w, conv_b, w_down):
    b, s, d = x_prompt.shape
    db, t, _ = x_sample.shape
    depth = w_in.shape[0]
    dff = w_gate.shape[-1]
    n_mem = mem_prompt.shape[1]
    past_len = page_table.shape[1] * cache_k.shape[2]
    n_state = state_pool.shape[2]
    rope_p = _rope_tables(jnp.arange(s, dtype=jnp.int32))
    rope_s = _rope_tables(jnp.tile(past_len + jnp.arange(t, dtype=jnp.int32), db))
    yp = x_prompt.reshape(b * s, d)
    ys = x_sample.reshape(db * t, d)
    outs = [[] for _ in range(10)]
    for l in range(depth):
        lam_init = 0.8 - 0.6 * math.exp(-0.3 * l)
        w = {'attn_norm_w': attn_norm_w[l], 'w_in': w_in[l], 'q_norm_w': q_norm_w[l],
             'k_norm_w': k_norm_w[l], 'xq_norm_w': xq_norm_w[l],
             'w_branch_attn': w_branch_attn[l].astype(BF16), 'w_branch_pool': w_branch_pool[l].astype(BF16),
             'w_branch_cross': w_branch_cross[l].astype(BF16), 'w_out': w_out[l].astype(BF16),
             'ffn_norm_w': ffn_norm_w[l], 'w_gate': w_gate[l], 'w_up': w_up[l],
             'conv_w': conv_w[l], 'conv_b': conv_b[l], 'w_down': w_down[l].astype(BF16)}
        lam_vecs = [v[l].reshape(1, HEAD_DIM).astype(F32) for v in (lambda_q1, lambda_k1, lambda_q2, lambda_k2)]
        pw = pool_w[l].astype(BF16)
        ps = pool_scale[l]
        sw = subln_w[l]

        mn = _rmsnorm(mem_prompt.reshape(b * n_mem, d), mem_norm_w[l], "mem_norm")
        tm_m = _row_tile(b * n_mem, 1024)
        mk32, mkbf = _proj(mn, w_mem_k[l], 0, X_W, [F32, BF16], "proj_mem_k", tm=tm_m,
                           norm_w=xk_norm_w[l])
        mv32, mvbf = _proj(mn, w_mem_v[l], 0, X_W, [F32, BF16], "proj_mem_v", tm=tm_m)
        tm_p = _row_tile(s, 1024)
        yp, k_p, v_p, pz_p, gl_p = _layer(
            yp, w, lam_init, tm=tm_p, seq_len=s, rope=rope_p,
            attn_fn=lambda q, k, v: _attn_prompt(q, k, v, lam_vecs, sw, b, s, lam_init),
            pool_fn=lambda pz: _pool_prompt(pz, pw, ps, s),
            xattn_fn=lambda xq: _xattn_prompt(xq, mkbf, mvbf, b, s, n_mem),
            conv_prev=None, q_dtype=BF16)
        tiles_per_seq = s // tm_p
        conv_p = gl_p.reshape(b, tiles_per_seq, CONV_HALO, dff)[:, -1, CONV_HALO - (CONV_W - 1):]

        sc = state_conv[l]
        zrow = jnp.zeros((db, 1, dff), F32)
        p1 = jnp.concatenate([sc[:, 1:2]] + [zrow] * (t - 1), axis=1).reshape(db * t, dff)
        p2 = jnp.concatenate([sc[:, 0:1], sc[:, 1:2]] + [zrow] * (t - 2), axis=1).reshape(db * t, dff)
        sp = state_pool[l]
        ys, k_s, v_s, pz_s, g_s = _layer(
            ys, w, lam_init, tm=_row_tile(db * t, 1024), seq_len=t, rope=rope_s,
            attn_fn=lambda q, k, v: _attn_sample(q, k, v, cache_k, cache_v, l, page_table, lam_vecs, sw, db, t,
                                                 lam_init).astype(BF16),
            pool_fn=lambda pz: _pool_sample(pz, sp, pw, ps, db, t, past_len),
            xattn_fn=lambda xq: _xattn_sample(xq, cache_mem_k, cache_mem_v, l, db, t),
            conv_prev=(p1, p2), q_dtype=F32)
        pool_s = jnp.concatenate([sp, pz_s.reshape(db, t, POOL_W)], axis=1)[:, -n_state:]
        conv_s = jnp.concatenate([sc, g_s.reshape(db, t, dff)], axis=1)[:, -(CONV_W - 1):]

        vals = [k_p.reshape(b, s, N_HEADS, V_DIM), v_p.reshape(b, s, N_HEADS, V_DIM),
                mk32.reshape(b, n_mem, X_HEADS, X_HEAD_DIM), mv32.reshape(b, n_mem, X_HEADS, X_HEAD_DIM),
                pz_p.reshape(b, s, POOL_W)[:, -n_state:], conv_p,
                k_s.reshape(db, t, N_HEADS, V_DIM), v_s.reshape(db, t, N_HEADS, V_DIM), pool_s, conv_s]
        for o, v in zip(outs, vals):
            o.append(v)
    return (yp.reshape(b, s, d), ys.reshape(db, t, d)) + tuple(jnp.stack(o) for o in outs)
```

```python
import functools
import math

import jax
import jax.numpy as jnp
from jax import lax
from jax.experimental import pallas as pl
from jax.experimental.pallas import tpu as pltpu

F32 = jnp.float32
BF16 = jnp.bfloat16

N_HEADS = 8
HEAD_DIM = 128
V_DIM = 2 * HEAD_DIM
ATTN_W = N_HEADS * V_DIM
POOL_WINDOWS = (2, 4, 8, 16)
POOL_GC = 256
POOL_W = len(POOL_WINDOWS) * POOL_GC
X_HEADS = 4
X_HEAD_DIM = 256
X_W = X_HEADS * X_HEAD_DIM
CONV_W = 3
ROPE_THETA = 10000.0
EPS = 1e-6
NEG = -1e30
LOG2_E = math.log2(math.e)

SUBLANES = 8
BF16_ROWS = 16
LANES = 128
PAGES_PER_STEP = 8
XATTN_GROUP = 4
MIB = 2 ** 20
STREAM_VMEM_MIB = 56


def _params(sem, vmem_mib):
    return pltpu.CompilerParams(dimension_semantics=sem, vmem_limit_bytes=vmem_mib * MIB)


def _row_tile(m, cap):
    t = min(m, cap)
    assert m % t == 0, (m, t)
    return t


def _dot(a, b):
    return jnp.dot(a, b, preferred_element_type=F32)


def _dot_nt(a, b):
    return lax.dot_general(a, b, (((1,), (1,)), ((), ())), preferred_element_type=F32)


def _rmsnorm_body(x_ref, w_ref, o_ref):
    x = x_ref[...]
    ms = jnp.mean(x * x, axis=-1, keepdims=True)
    o_ref[...] = (x * lax.rsqrt(ms + EPS) * w_ref[...]).astype(o_ref.dtype)


def _rmsnorm(x, w, name):
    m, d = x.shape
    tm = _row_tile(m, 256)
    return pl.pallas_call(
        _rmsnorm_body,
        out_shape=jax.ShapeDtypeStruct((m, d), BF16),
        grid=(m // tm,),
        in_specs=[pl.BlockSpec((tm, d), lambda i: (i, 0)), pl.BlockSpec((1, d), lambda i: (0, 0))],
        out_specs=pl.BlockSpec((tm, d), lambda i: (i, 0)),
        compiler_params=_params(("parallel",), 32),
        name=name,
    )(x, w.reshape(1, d))


def _tile_maps(ni, nj, lag):
    def index(s):
        t = jnp.clip(s - ni - lag, 0, ni * nj - 1)
        return t % ni, t // ni

    return index


def _cast_weight_chunk(s, ni, w_ref, wbf):
    kc = w_ref.shape[0]
    slot = (s // ni) % 2
    r0 = pl.multiple_of((s % ni) * kc, kc)
    wbf[slot, pl.ds(r0, kc), :] = w_ref[...].astype(BF16)
    return 1 - slot


def _proj_body(*refs, ni, tn, chunk, rotary, sigmoid, n_out, delayed):
    x_ref, w_ref = refs[0], refs[1]
    idx = 2
    nw_ref = cos_ref = sin_ref = None
    if chunk:
        nw_ref = refs[idx]
        idx += 1
    if rotary:
        cos_ref, sin_ref = refs[idx], refs[idx + 1]
        idx += 2
    outs = refs[idx:idx + n_out]
    scratch = refs[idx + n_out:]
    wbf = scratch[0]
    s = pl.program_id(0)
    cur = _cast_weight_chunk(s, ni, w_ref, wbf)

    def epilogue(z):
        if sigmoid:
            z = jax.nn.sigmoid(z)
        if not chunk:
            for o in outs:
                o[...] = z.astype(o.dtype)
            return
        nw = nw_ref[...]
        for c in range(tn // chunk):
            cols = slice(c * chunk, (c + 1) * chunk)
            zc = z[:, cols]
            ms = jnp.mean(zc * zc, axis=-1, keepdims=True)
            y = zc * lax.rsqrt(ms + EPS) * nw
            if rotary:
                y = y * cos_ref[...] + pltpu.roll(y, chunk // 2, 1) * sin_ref[...]
            for o in outs:
                o[:, cols] = y.astype(o.dtype)

    if not delayed:
        @pl.when(s >= ni)
        def _():
            epilogue(_dot(x_ref[...], wbf[cur]))
        return

    za, zb = scratch[1:]
    t = s - ni

    @pl.when(t == 0)
    def _():
        zb[...] = jnp.zeros(zb.shape, F32)

    def step(z_w, z_r):
        z_w[...] = _dot(x_ref[...], wbf[cur])
        epilogue(z_r[...])

    @pl.when((t >= 0) & (t % 2 == 0))
    def _():
        step(za, zb)

    @pl.when((t >= 0) & (t % 2 == 1))
    def _():
        step(zb, za)


def _proj(x, w, col0, ncols, out_dtypes, name, *, tm, norm_w=None, rope=None, rope_rep=1, sigmoid=False,
          tn=512):
    m, k = x.shape
    assert m % tm == 0 and ncols % tn == 0 and col0 % tn == 0
    cb0 = col0 // tn
    ni, nj = m // tm, ncols // tn
    assert k % (ni * BF16_ROWS) == 0
    chunk = 0 if norm_w is None else norm_w.shape[-1]
    delayed = bool(chunk)
    mm = _tile_maps(ni, nj, 0)
    ep = _tile_maps(ni, nj, int(delayed))
    ins = [x, w]
    in_specs = [pl.BlockSpec((tm, k), lambda s: (mm(s)[0], 0)),
                pl.BlockSpec((k // ni, tn), lambda s: (s % ni, cb0 + jnp.minimum(s // ni, nj - 1)))]
    if chunk:
        ins.append(norm_w.reshape(1, chunk).astype(F32))
        in_specs.append(pl.BlockSpec((1, chunk), lambda s: (0, 0)))
    if rope is not None:
        for t in rope:
            ins.append(t)
            in_specs.append(pl.BlockSpec((tm, chunk), lambda s: (ep(s)[0] % rope_rep, 0)))
    out_shape = [jax.ShapeDtypeStruct((m, ncols), dt) for dt in out_dtypes]
    out_specs = [pl.BlockSpec((tm, tn), lambda s: ep(s)) for _ in out_dtypes]
    body = functools.partial(_proj_body, ni=ni, tn=tn, chunk=chunk, rotary=rope is not None, sigmoid=sigmoid,
                             n_out=len(out_dtypes), delayed=delayed)
    scratch = [pltpu.VMEM((2, k, tn), BF16)]
    if delayed:
        scratch += [pltpu.VMEM((tm, tn), F32), pltpu.VMEM((tm, tn), F32)]
    res = pl.pallas_call(
        body,
        out_shape=out_shape,
        grid=((nj + 1) * ni + int(delayed),),
        in_specs=in_specs,
        out_specs=out_specs,
        scratch_shapes=scratch,
        compiler_params=_params(("arbitrary",), STREAM_VMEM_MIB),
        name=name,
    )(*ins)
    return res


def _lam_value(lq1, lk1, lq2, lk2, lam_init):
    a = jnp.sum(lq1[...] * lk1[...], axis=-1, keepdims=True)
    b = jnp.sum(lq2[...] * lk2[...], axis=-1, keepdims=True)
    return jnp.exp(a) - jnp.exp(b) + lam_init


def _subln(o, sw, lam_init):
    ms = jnp.mean(o * o, axis=-1, keepdims=True)
    return (o * lax.rsqrt(ms + EPS) * sw) * (1.0 - lam_init)


def _attn_body(lq1, lk1, lq2, lk2, sw_ref, q_ref, k_ref, v_ref, o_ref,
               m1, l1, a1, m2, l2, a2, *, tq, lam_init, scale):
    qi = pl.program_id(2)
    q = q_ref[...]
    qs = (q[:, :HEAD_DIM], q[:, HEAD_DIM:])
    stats = ((m1, l1, a1), (m2, l2, a2))
    for m, l, a in stats:
        m[...] = jnp.full(m.shape, NEG, F32)
        l[...] = jnp.zeros(l.shape, F32)
        a[...] = jnp.zeros(a.shape, F32)

    def scores(j):
        start = pl.multiple_of(j * tq, tq)
        kb = k_ref[pl.ds(start, tq), :]
        return tuple(_dot_nt(qs[c], kb[:, c * HEAD_DIM:(c + 1) * HEAD_DIM]) for c in range(2))

    def accumulate(j, ss, mask):
        start = pl.multiple_of(j * tq, tq)
        vb = v_ref[pl.ds(start, tq), :]
        for s, (m, l, a) in zip(ss, stats):
            if mask is not None:
                s = jnp.where(mask, s, NEG)
            m_old = m[...]
            m_new = jnp.maximum(m_old, jnp.max(s, axis=-1, keepdims=True))
            alpha = jnp.exp2((m_old - m_new) * (scale * LOG2_E))
            p = jnp.exp2((s - m_new) * (scale * LOG2_E))
            l[...] = alpha * l[...] + jnp.sum(p, axis=-1, keepdims=True)
            a[...] = alpha * a[...] + _dot(p.astype(BF16), vb)
            m[...] = m_new

    def body(j, ss):
        nxt = scores(j + 1)
        accumulate(j, ss, None)
        return nxt

    ss = lax.fori_loop(0, qi, body, scores(0))
    row = lax.broadcasted_iota(jnp.int32, (tq, tq), 0)
    col = lax.broadcasted_iota(jnp.int32, (tq, tq), 1)
    accumulate(qi, ss, col <= row)

    lam = _lam_value(lq1, lk1, lq2, lk2, lam_init)
    o = a1[...] / l1[...] - lam * (a2[...] / l2[...])
    o_ref[...] = _subln(o, sw_ref[...], lam_init).astype(o_ref.dtype)


def _attn_prompt(q, k, v, lam_vecs, subln_w, b, s, lam_init):
    tq = _row_tile(s, 512)
    nq = s // tq
    vec = pl.BlockSpec((1, HEAD_DIM), lambda bi, h, i: (0, 0))
    body = functools.partial(_attn_body, tq=tq, lam_init=lam_init, scale=HEAD_DIM ** -0.5)
    return pl.pallas_call(
        body,
        out_shape=jax.ShapeDtypeStruct((b * s, ATTN_W), BF16),
        grid=(b, N_HEADS, nq),
        in_specs=[vec, vec, vec, vec,
                  pl.BlockSpec((1, V_DIM), lambda bi, h, i: (0, 0)),
                  pl.BlockSpec((tq, V_DIM), lambda bi, h, i: (bi * nq + i, h)),
                  pl.BlockSpec((s, V_DIM), lambda bi, h, i: (bi, h)),
                  pl.BlockSpec((s, V_DIM), lambda bi, h, i: (bi, h))],
        out_specs=pl.BlockSpec((tq, V_DIM), lambda bi, h, i: (bi * nq + i, h)),
        scratch_shapes=[pltpu.VMEM((tq, 1), F32), pltpu.VMEM((tq, 1), F32), pltpu.VMEM((tq, V_DIM), F32),
                        pltpu.VMEM((tq, 1), F32), pltpu.VMEM((tq, 1), F32), pltpu.VMEM((tq, V_DIM), F32)],
        compiler_params=_params(("parallel", "parallel", "arbitrary"), 40),
        name="attn_prompt",
    )(*lam_vecs, subln_w.reshape(1, V_DIM), q, k, v)


def _sattn_body(pt_ref, lq1, lk1, lq2, lk2, sw_ref, q_ref, kn_ref, vn_ref, *rest, npg, t_pad, lam_init, scale):
    del pt_ref
    k_refs = rest[:npg]
    v_refs = rest[npg:2 * npg]
    o_ref = rest[2 * npg]
    m_ref, l_ref, acc_ref = rest[2 * npg + 1:]
    c = pl.program_id(1)
    q = q_ref[...].astype(BF16)
    nrows = q.shape[0]

    def masks(ncols):
        row = lax.broadcasted_iota(jnp.int32, (nrows, ncols), 0)
        col = lax.broadcasted_iota(jnp.int32, (nrows, ncols), 1)
        same_head = (row // (2 * t_pad)) == (col % N_HEADS)
        return same_head, row % t_pad, col // N_HEADS

    def update(kbs, vbs, mask):
        ss = [jnp.where(mask, _dot_nt(q, kb) * scale, NEG) for kb in kbs]
        m_cur = functools.reduce(jnp.maximum, [jnp.max(s, axis=-1, keepdims=True) for s in ss])
        m_old = m_ref[...]
        m_new = jnp.maximum(m_old, m_cur)
        alpha = jnp.exp(m_old - m_new)
        ps = [jnp.exp(s - m_new) for s in ss]
        l_ref[...] = alpha * l_ref[...] + sum(jnp.sum(p, axis=-1, keepdims=True) for p in ps)
        acc_ref[...] = alpha * acc_ref[...] + sum(_dot(p.astype(BF16), vb) for p, vb in zip(ps, vbs))
        m_ref[...] = m_new

    @pl.when(c == 0)
    def _():
        m_ref[...] = jnp.full(m_ref.shape, NEG, F32)
        l_ref[...] = jnp.zeros(l_ref.shape, F32)
        acc_ref[...] = jnp.zeros(acc_ref.shape, F32)
        same_head, step, tok = masks(kn_ref.shape[0])
        update([kn_ref[...]], [vn_ref[...]], same_head & (tok <= step))

    update([r[...].astype(BF16) for r in k_refs], [r[...].astype(BF16) for r in v_refs],
           masks(k_refs[0].shape[0])[0])

    @pl.when(c == pl.num_programs(1) - 1)
    def _():
        lam = _lam_value(lq1, lk1, lq2, lk2, lam_init)
        y = acc_ref[...] / l_ref[...]
        o = y - lam * pltpu.roll(y, nrows - t_pad, 0)
        o_ref[...] = _subln(o, sw_ref[...], lam_init)


def _attn_sample(q32, k_bf, v_bf, cache_k, cache_v, layer, page_table, lam_vecs, subln_w, db, t, lam_init):
    t_pad = SUBLANES // 2
    assert t <= t_pad
    depth, n_phys, page, _, _ = cache_k.shape
    n_pages = page_table.shape[1]
    npg = math.gcd(PAGES_PER_STEP, n_pages)
    ck = cache_k.reshape(depth, n_phys, page * N_HEADS, V_DIM)
    cv = cache_v.reshape(depth, n_phys, page * N_HEADS, V_DIM)
    q5 = q32.reshape(db, t, N_HEADS, 2, HEAD_DIM).transpose(0, 2, 3, 1, 4)
    q5 = jnp.pad(q5, ((0, 0), (0, 0), (0, 0), (0, t_pad - t), (0, 0)))
    eye = jnp.eye(2, dtype=F32)
    nrows = N_HEADS * 2 * t_pad
    qbd = (q5[:, :, :, :, None, :] * eye[None, None, :, None, :, None]).reshape(db, nrows, V_DIM)
    new_tok = LANES // N_HEADS
    pad = ((0, 0), (0, new_tok - t), (0, 0), (0, 0))
    kn = jnp.pad(k_bf.reshape(db, t, N_HEADS, V_DIM), pad).reshape(db, LANES, V_DIM)
    vn = jnp.pad(v_bf.reshape(db, t, N_HEADS, V_DIM), pad).reshape(db, LANES, V_DIM)

    def page_spec(i):
        return pl.BlockSpec((None, None, page * N_HEADS, V_DIM),
                            lambda b, c, pt: (layer, pt[b, c * npg + i], 0, 0))

    vec = pl.BlockSpec((1, HEAD_DIM), lambda b, c, pt: (0, 0))
    per_seq = lambda rows: pl.BlockSpec((None, rows, V_DIM), lambda b, c, pt: (b, 0, 0))
    in_specs = [vec, vec, vec, vec, pl.BlockSpec((1, V_DIM), lambda b, c, pt: (0, 0)),
                per_seq(nrows), per_seq(LANES), per_seq(LANES)]
    in_specs += [page_spec(i) for i in range(npg)] * 2
    body = functools.partial(_sattn_body, npg=npg, t_pad=t_pad, lam_init=lam_init, scale=HEAD_DIM ** -0.5)
    out = pl.pallas_call(
        body,
        out_shape=jax.ShapeDtypeStruct((db, nrows, V_DIM), F32),
        grid_spec=pltpu.PrefetchScalarGridSpec(
            num_scalar_prefetch=1,
            grid=(db, n_pages // npg),
            in_specs=in_specs,
            out_specs=per_seq(nrows),
            scratch_shapes=[pltpu.VMEM((nrows, 1), F32), pltpu.VMEM((nrows, 1), F32),
                            pltpu.VMEM((nrows, V_DIM), F32)]),
        compiler_params=_params(("arbitrary", "arbitrary"), 48),
        name="attn_sample",
    )(page_table, *lam_vecs, subln_w.reshape(1, V_DIM), qbd, kn, vn, *([ck] * npg), *([cv] * npg))
    out = out.reshape(db, N_HEADS, 2, t_pad, V_DIM)[:, :, 0, :t]
    return out.transpose(0, 2, 1, 3).reshape(db * t, ATTN_W)


def _xattn_prompt_body(q_ref, k_ref, v_ref, o_ref, *, scale):
    s = _dot_nt(q_ref[...], k_ref[...]) * scale
    m = jnp.max(s, axis=-1, keepdims=True)
    p = jnp.exp(s - m)
    p = p / jnp.sum(p, axis=-1, keepdims=True)
    o_ref[...] = _dot(p.astype(BF16), v_ref[...]).astype(o_ref.dtype)


def _xattn_prompt(xq, mk, mv, b, s, n_mem):
    tq = _row_tile(s, 1024)
    nq = s // tq
    body = functools.partial(_xattn_prompt_body, scale=X_HEAD_DIM ** -0.5)
    return pl.pallas_call(
        body,
        out_shape=jax.ShapeDtypeStruct((b * s, X_W), BF16),
        grid=(b, X_HEADS, nq),
        in_specs=[pl.BlockSpec((tq, X_HEAD_DIM), lambda bi, h, i: (bi * nq + i, h)),
                  pl.BlockSpec((n_mem, X_HEAD_DIM), lambda bi, h, i: (bi, h)),
                  pl.BlockSpec((n_mem, X_HEAD_DIM), lambda bi, h, i: (bi, h))],
        out_specs=pl.BlockSpec((tq, X_HEAD_DIM), lambda bi, h, i: (bi * nq + i, h)),
        compiler_params=_params(("parallel", "parallel", "parallel"), 32),
        name="xattn_prompt",
    )(xq, mk, mv)


def _xattn_sample_body(q_ref, k_ref, v_ref, o_ref, *, t, scale):
    g, n_mem = k_ref.shape[0], k_ref.shape[1]
    rows = q_ref.shape[0]
    ncols = g * n_mem * X_HEADS
    kb = k_ref[...].reshape(ncols, X_HEAD_DIM).astype(BF16)
    vb = v_ref[...].reshape(ncols, X_HEAD_DIM).astype(BF16)
    row = lax.broadcasted_iota(jnp.int32, (rows, ncols), 0)
    col = lax.broadcasted_iota(jnp.int32, (rows, ncols), 1)
    same = ((row // (t * X_HEADS)) == (col // (n_mem * X_HEADS))) & ((row % X_HEADS) == (col % X_HEADS))
    s = jnp.where(same, _dot_nt(q_ref[...], kb) * scale, NEG)
    m = jnp.max(s, axis=-1, keepdims=True)
    p = jnp.exp(s - m)
    p = p / jnp.sum(p, axis=-1, keepdims=True)
    o_ref[...] = _dot(p.astype(BF16), vb).astype(o_ref.dtype)


def _xattn_sample(xq, mem_k, mem_v, layer, db, t):
    n_mem = mem_k.shape[2]
    g = math.gcd(XATTN_GROUP, db)
    rows = g * t * X_HEADS
    q = xq.reshape(db * t * X_HEADS, X_HEAD_DIM)
    mem_spec = pl.BlockSpec((None, g, n_mem, X_HEADS, X_HEAD_DIM), lambda i: (layer, i, 0, 0, 0))
    body = functools.partial(_xattn_sample_body, t=t, scale=X_HEAD_DIM ** -0.5)
    out = pl.pallas_call(
        body,
        out_shape=jax.ShapeDtypeStruct((db * t * X_HEADS, X_HEAD_DIM), BF16),
        grid=(db // g,),
        in_specs=[pl.BlockSpec((rows, X_HEAD_DIM), lambda i: (i, 0)), mem_spec, mem_spec],
        out_specs=pl.BlockSpec((rows, X_HEAD_DIM), lambda i: (i, 0)),
        compiler_params=_params(("parallel",), 40),
        name="xattn_sample",
    )(q, mem_k, mem_v)
    return out.reshape(db * t, X_W)


POOL_HALO = 16


def _pool_prompt_body(x_ref, halo_ref, pw_ref, ps_ref, o_ref, buf, *, tm, tiles_per_seq):
    i = pl.program_id(0)
    first = (i % tiles_per_seq) == 0
    x = x_ref[...]
    buf[0:POOL_HALO, :] = jnp.where(first, 0.0, halo_ref[...])
    buf[POOL_HALO:, :] = x
    pos = (i % tiles_per_seq) * tm + lax.broadcasted_iota(jnp.int32, (tm, 1), 0)
    for g, w in enumerate(POOL_WINDOWS):
        cols = slice(g * POOL_GC, (g + 1) * POOL_GC)
        xs = x[:, cols]
        acc = xs
        for j in range(1, w):
            acc = acc + buf[POOL_HALO - j:POOL_HALO - j + tm, cols]
        cnt = jnp.minimum(w, pos + 1).astype(F32)
        d = acc / cnt - xs
        o_ref[:, cols] = (_dot(d.astype(BF16), pw_ref[g]) * ps_ref[:, cols]).astype(o_ref.dtype)


def _pool_prompt(pz, pool_w, pool_scale, s):
    m = pz.shape[0]
    tm = _row_tile(s, 512)
    hb = tm // POOL_HALO
    body = functools.partial(_pool_prompt_body, tm=tm, tiles_per_seq=s // tm)
    return pl.pallas_call(
        body,
        out_shape=jax.ShapeDtypeStruct((m, POOL_W), BF16),
        grid=(m // tm,),
        in_specs=[pl.BlockSpec((tm, POOL_W), lambda i: (i, 0)),
                  pl.BlockSpec((POOL_HALO, POOL_W), lambda i: (jnp.maximum(i * hb - 1, 0), 0)),
                  pl.BlockSpec(pool_w.shape, lambda i: (0, 0, 0)),
                  pl.BlockSpec((1, POOL_W), lambda i: (0, 0))],
        out_specs=pl.BlockSpec((tm, POOL_W), lambda i: (i, 0)),
        scratch_shapes=[pltpu.VMEM((tm + POOL_HALO, POOL_W), F32)],
        compiler_params=_params(("parallel",), 32),
        name="pool_prompt",
    )(pz, pz, pool_w, pool_scale.reshape(1, POOL_W))


def _pool_sample_body(st_ref, pz_ref, pw_ref, ps_ref, o_ref, *, n_state, t, past_len):
    def row(idx, cols):
        return st_ref[idx][:, cols] if idx < n_state else pz_ref[idx - n_state][:, cols]

    for g, w in enumerate(POOL_WINDOWS):
        cols = slice(g * POOL_GC, (g + 1) * POOL_GC)
        ds = []
        for ti in range(t):
            hi = n_state + ti + 1
            lo = max(hi - w, 0)
            acc = row(hi - 1, cols)
            for idx in range(hi - 2, lo - 1, -1):
                acc = acc + row(idx, cols)
            cnt = float(min(w, past_len + ti + 1))
            ds.append(acc / cnt - row(hi - 1, cols))
        d = jnp.concatenate(ds, axis=0)
        o_ref[:, cols] = (_dot(d.astype(BF16), pw_ref[g]) * ps_ref[:, cols]).astype(o_ref.dtype)


def _pool_sample(pz, state_pool, pool_w, pool_scale, db, t, past_len):
    n_state = state_pool.shape[1]
    st = state_pool.transpose(1, 0, 2)
    pzt = pz.reshape(db, t, POOL_W).transpose(1, 0, 2)
    body = functools.partial(_pool_sample_body, n_state=n_state, t=t, past_len=past_len)
    out = pl.pallas_call(
        body,
        out_shape=jax.ShapeDtypeStruct((t * db, POOL_W), BF16),
        grid=(1,),
        in_specs=[pl.BlockSpec(st.shape, lambda i: (0, 0, 0)),
                  pl.BlockSpec(pzt.shape, lambda i: (0, 0, 0)),
                  pl.BlockSpec(pool_w.shape, lambda i: (0, 0, 0)),
                  pl.BlockSpec((1, POOL_W), lambda i: (0, 0))],
        out_specs=pl.BlockSpec((t * db, POOL_W), lambda i: (0, 0)),
        compiler_params=_params(("arbitrary",), 40),
        name="pool_sample",
    )(st, pzt, pool_w, pool_scale.reshape(1, POOL_W))
    return out.reshape(t, db, POOL_W).transpose(1, 0, 2).reshape(db * t, POOL_W)


def _merge_body(a_ref, p_ref, c_ref, wa_ref, wp_ref, wc_ref, g0_ref, g1_ref, g2_ref, o_ref):
    m = (g0_ref[...] * _dot(a_ref[...], wa_ref[...])
         + g1_ref[...] * _dot(p_ref[...], wp_ref[...])
         + g2_ref[...] * _dot(c_ref[...], wc_ref[...]))
    o_ref[...] = m.astype(o_ref.dtype)


def _merge(a, po, c, gates, wa, wp, wc, d, tm):
    m = a.shape[0]
    assert m % tm == 0
    tn = 512
    nj = d // tn
    row = lambda k: pl.BlockSpec((tm, k), lambda i, j: (i, 0))
    wcol = lambda k: pl.BlockSpec((k, tn), lambda i, j: (0, j))
    gate = lambda r: pl.BlockSpec((tm, tn), lambda i, j: (i, r * nj + j))
    return pl.pallas_call(
        _merge_body,
        out_shape=jax.ShapeDtypeStruct((m, d), BF16),
        grid=(m // tm, nj),
        in_specs=[row(ATTN_W), row(POOL_W), row(X_W), wcol(ATTN_W), wcol(POOL_W), wcol(X_W),
                  gate(0), gate(1), gate(2)],
        out_specs=pl.BlockSpec((tm, tn), lambda i, j: (i, j)),
        compiler_params=_params(("parallel", "arbitrary"), 48),
        name="merge",
    )(a, po, c, wa, wp, wc, gates, gates, gates)


def _resid_body(x_ref, w_ref, r_ref, o_ref):
    o_ref[...] = r_ref[...] + _dot(x_ref[...], w_ref[...])


def _resid_proj(x, w, resid, name, *, tm, tn):
    m, k = x.shape
    n = w.shape[1]
    assert m % tm == 0 and n % tn == 0
    return pl.pallas_call(
        _resid_body,
        out_shape=jax.ShapeDtypeStruct((m, n), F32),
        grid=(m // tm, n // tn),
        in_specs=[pl.BlockSpec((tm, k), lambda i, j: (i, 0)),
                  pl.BlockSpec((k, tn), lambda i, j: (0, j)),
                  pl.BlockSpec((tm, tn), lambda i, j: (i, j))],
        out_specs=pl.BlockSpec((tm, tn), lambda i, j: (i, j)),
        compiler_params=_params(("parallel", "arbitrary"), STREAM_VMEM_MIB),
        name=name,
    )(x, w, resid)


CONV_HALO = SUBLANES


def _ffn_a_body(*refs, ni, tm, tiles_per_seq, seq_len, has_prev):
    h_ref, wg_ref, wu_ref, cw_ref, cb_ref = refs[:5]
    idx = 5
    if has_prev:
        p1_ref, p2_ref = refs[idx:idx + 2]
        idx += 2
    act_ref, g_ref, wgb, wub, buf, carry = refs[idx:]
    s = pl.program_id(0)
    cur = _cast_weight_chunk(s, ni, wg_ref, wgb)
    _cast_weight_chunk(s, ni, wu_ref, wub)

    @pl.when(s == 0)
    def _():
        carry[...] = jnp.zeros(carry.shape, F32)

    @pl.when(s >= ni)
    def _():
        _ffn_a_tile(s, cur, h_ref, wgb, wub, cw_ref, cb_ref, p1_ref if has_prev else None,
                    p2_ref if has_prev else None, act_ref, g_ref, buf, carry, ni=ni, tm=tm,
                    tiles_per_seq=tiles_per_seq, seq_len=seq_len)


def _ffn_a_tile(s, cur, h_ref, wgb, wub, cw_ref, cb_ref, p1_ref, p2_ref, act_ref, g_ref, buf, carry, *,
                ni, tm, tiles_per_seq, seq_len):
    has_prev = p1_ref is not None
    h = h_ref[...]
    g = _dot(h, wgb[cur])
    u = _dot(h, wub[cur])
    if tiles_per_seq:
        first = ((s % ni) % tiles_per_seq) == 0
        buf[0:CONV_HALO, :] = jnp.where(first, 0.0, carry[...])
    else:
        buf[0:CONV_HALO, :] = jnp.zeros((CONV_HALO, g.shape[1]), F32)
    buf[CONV_HALO:, :] = g
    carry[...] = g[tm - CONV_HALO:, :]
    g1 = buf[CONV_HALO - 1:CONV_HALO - 1 + tm, :]
    g2 = buf[CONV_HALO - 2:CONV_HALO - 2 + tm, :]
    if not tiles_per_seq:
        t = lax.broadcasted_iota(jnp.int32, (tm, 1), 0) % seq_len
        g1 = jnp.where(t >= 1, g1, 0.0)
        g2 = jnp.where(t >= 2, g2, 0.0)
    if has_prev:
        g1 = g1 + p1_ref[...]
        g2 = g2 + p2_ref[...]
    cw = cw_ref[...]
    gc = cb_ref[...] + g2 * cw[0:1, :]
    gc = gc + g1 * cw[1:2, :]
    gc = gc + g * cw[2:3, :]
    act_ref[...] = (jax.nn.silu(gc) * u).astype(act_ref.dtype)
    if tiles_per_seq:
        g_ref[...] = g[tm - CONV_HALO:, :]
    else:
        g_ref[...] = g


def _ffn_a(h2, wg, wu, conv_w, conv_b, seq_len, tm, prev=None):
    m, d = h2.shape
    dff = wg.shape[1]
    tn = 256
    assert dff % tn == 0 and m % tm == 0
    long_seq = seq_len >= tm
    if long_seq:
        assert seq_len % tm == 0 and prev is None
        tiles_per_seq = seq_len // tm
    else:
        assert tm == m and tm % seq_len == 0
        tiles_per_seq = 0
    ni, nj = m // tm, dff // tn
    assert d % (ni * BF16_ROWS) == 0
    tile = _tile_maps(ni, nj, 0)
    col = lambda s: (0, tile(s)[1])
    w_chunk = pl.BlockSpec((d // ni, tn), lambda s: (s % ni, jnp.minimum(s // ni, nj - 1)))
    ins = [h2, wg, wu, conv_w, conv_b.reshape(1, dff)]
    in_specs = [pl.BlockSpec((tm, d), lambda s: (tile(s)[0], 0)),
                w_chunk,
                w_chunk,
                pl.BlockSpec((CONV_W, tn), col),
                pl.BlockSpec((1, tn), col)]
    if prev is not None:
        ins += list(prev)
        in_specs += [pl.BlockSpec((tm, tn), tile)] * 2
    if long_seq:
        g_shape = jax.ShapeDtypeStruct((ni, CONV_HALO, dff), F32)
        g_spec = pl.BlockSpec((None, CONV_HALO, tn), lambda s: (tile(s)[0], 0, tile(s)[1]))
    else:
        g_shape = jax.ShapeDtypeStruct((m, dff), F32)
        g_spec = pl.BlockSpec((tm, tn), tile)
    body = functools.partial(_ffn_a_body, ni=ni, tm=tm, tiles_per_seq=tiles_per_seq, seq_len=seq_len,
                             has_prev=prev is not None)
    return pl.pallas_call(
        body,
        out_shape=[jax.ShapeDtypeStruct((m, dff), BF16), g_shape],
        grid=((nj + 1) * ni,),
        in_specs=in_specs,
        out_specs=[pl.BlockSpec((tm, tn), tile), g_spec],
        scratch_shapes=[pltpu.VMEM((2, d, tn), BF16), pltpu.VMEM((2, d, tn), BF16),
                        pltpu.VMEM((tm + CONV_HALO, tn), F32), pltpu.VMEM((CONV_HALO, tn), F32)],
        compiler_params=_params(("arbitrary",), STREAM_VMEM_MIB),
        name="ffn_gate_up",
    )(*ins)


def _rope_tables(pos):
    half = HEAD_DIM // 2
    inv = ROPE_THETA ** (-jnp.arange(half, dtype=F32) / half)
    ang = pos.astype(F32)[:, None] * inv[None, :]
    cos, sin = jnp.cos(ang), jnp.sin(ang)
    return jnp.concatenate([cos, cos], axis=-1), jnp.concatenate([-sin, sin], axis=-1)


def _layer(x2, w, lam_init, *, tm, seq_len, rope, attn_fn, pool_fn, xattn_fn, conv_prev, q_dtype):
    d = x2.shape[1]
    rope_rep = rope[0].shape[0] // tm
    pool_off = 3 * ATTN_W
    xq_off = pool_off + POOL_W
    gate_off = xq_off + X_W
    h = _rmsnorm(x2, w['attn_norm_w'], "attn_norm")
    win = w['w_in']
    (q,) = _proj(h, win, 0, ATTN_W, [q_dtype], "proj_q", tm=tm, norm_w=w['q_norm_w'], rope=rope,
                 rope_rep=rope_rep)
    k32, kbf = _proj(h, win, ATTN_W, ATTN_W, [F32, BF16], "proj_k", tm=tm, norm_w=w['k_norm_w'], rope=rope,
                     rope_rep=rope_rep)
    v32, vbf = _proj(h, win, 2 * ATTN_W, ATTN_W, [F32, BF16], "proj_v", tm=tm)
    (pz,) = _proj(h, win, pool_off, POOL_W, [F32], "proj_pool", tm=tm)
    (xq,) = _proj(h, win, xq_off, X_W, [BF16], "proj_xq", tm=tm, norm_w=w['xq_norm_w'])
    (gates,) = _proj(h, win, gate_off, 3 * d, [F32], "proj_gates", tm=tm, sigmoid=True)
    a = attn_fn(q, kbf, vbf)
    po = pool_fn(pz)
    c = xattn_fn(xq)
    m = _merge(a, po, c, gates, w['w_branch_attn'], w['w_branch_pool'], w['w_branch_cross'], d, tm)
    x1 = _resid_proj(m, w['w_out'], x2, "out_proj", tm=tm, tn=512)
    h2 = _rmsnorm(x1, w['ffn_norm_w'], "ffn_norm")
    act, gl = _ffn_a(h2, w['w_gate'], w['w_up'], w['conv_w'], w['conv_b'], seq_len, tm, conv_prev)
    y = _resid_proj(act, w['w_down'], x1, "down_proj", tm=min(tm, 512), tn=512)
    return y, k32, v32, pz, gl


def kernel(x_prompt, x_sample, cache_k, cache_v, cache_mem_k, cache_mem_v, state_pool, state_conv, page_table, mem_prompt, attn_norm_w, w_in, q_norm_w, k_norm_w, lambda_q1, lambda_k1, lambda_q2, lambda_k2, subln_w, pool_w, pool_scale, mem_norm_w, w_mem_k, w_mem_v, xq_norm_w, xk_norm_w, w_branch_attn, w_branch_pool, w_branch_cross, w_out, ffn_norm_w, w_gate, w_up, conv_w, conv_b, w_down):
    b, s, d = x_prompt.shape
    db, t, _ = x_sample.shape
    depth = w_in.shape[0]
    dff = w_gate.shape[-1]
    n_mem = mem_prompt.shape[1]
    past_len = page_table.shape[1] * cache_k.shape[2]
    n_state = state_pool.shape[2]
    rope_p = _rope_tables(jnp.arange(s, dtype=jnp.int32))
    rope_s = _rope_tables(jnp.tile(past_len + jnp.arange(t, dtype=jnp.int32), db))
    yp = x_prompt.reshape(b * s, d)
    ys = x_sample.reshape(db * t, d)
    outs = [[] for _ in range(10)]
    for l in range(depth):
        lam_init = 0.8 - 0.6 * math.exp(-0.3 * l)
        w = {'attn_norm_w': attn_norm_w[l], 'w_in': w_in[l], 'q_norm_w': q_norm_w[l],
             'k_norm_w': k_norm_w[l], 'xq_norm_w': xq_norm_w[l],
             'w_branch_attn': w_branch_attn[l].astype(BF16), 'w_branch_pool': w_branch_pool[l].astype(BF16),
             'w_branch_cross': w_branch_cross[l].astype(BF16), 'w_out': w_out[l].astype(BF16),
             'ffn_norm_w': ffn_norm_w[l], 'w_gate': w_gate[l], 'w_up': w_up[l],
             'conv_w': conv_w[l], 'conv_b': conv_b[l], 'w_down': w_down[l].astype(BF16)}
        lam_vecs = [v[l].reshape(1, HEAD_DIM).astype(F32) for v in (lambda_q1, lambda_k1, lambda_q2, lambda_k2)]
        pw = pool_w[l].astype(BF16)
        ps = pool_scale[l]
        sw = subln_w[l]

        mn = _rmsnorm(mem_prompt.reshape(b * n_mem, d), mem_norm_w[l], "mem_norm")
        tm_m = _row_tile(b * n_mem, 1024)
        mk32, mkbf = _proj(mn, w_mem_k[l], 0, X_W, [F32, BF16], "proj_mem_k", tm=tm_m,
                           norm_w=xk_norm_w[l])
        mv32, mvbf = _proj(mn, w_mem_v[l], 0, X_W, [F32, BF16], "proj_mem_v", tm=tm_m)
        tm_p = _row_tile(s, 1024)
        yp, k_p, v_p, pz_p, gl_p = _layer(
            yp, w, lam_init, tm=tm_p, seq_len=s, rope=rope_p,
            attn_fn=lambda q, k, v: _attn_prompt(q, k, v, lam_vecs, sw, b, s, lam_init),
            pool_fn=lambda pz: _pool_prompt(pz, pw, ps, s),
            xattn_fn=lambda xq: _xattn_prompt(xq, mkbf, mvbf, b, s, n_mem),
            conv_prev=None, q_dtype=BF16)
        tiles_per_seq = s // tm_p
        conv_p = gl_p.reshape(b, tiles_per_seq, CONV_HALO, dff)[:, -1, CONV_HALO - (CONV_W - 1):]

        sc = state_conv[l]
        zrow = jnp.zeros((db, 1, dff), F32)
        p1 = jnp.concatenate([sc[:, 1:2]] + [zrow] * (t - 1), axis=1).reshape(db * t, dff)
        p2 = jnp.concatenate([sc[:, 0:1], sc[:, 1:2]] + [zrow] * (t - 2), axis=1).reshape(db * t, dff)
        sp = state_pool[l]
        ys, k_s, v_s, pz_s, g_s = _layer(
            ys, w, lam_init, tm=_row_tile(db * t, 1024), seq_len=t, rope=rope_s,
            attn_fn=lambda q, k, v: _attn_sample(q, k, v, cache_k, cache_v, l, page_table, lam_vecs, sw, db, t,
                                                 lam_init).astype(BF16),
            pool_fn=lambda pz: _pool_sample(pz, sp, pw, ps, db, t, past_len),
            xattn_fn=lambda xq: _xattn_sample(xq, cache_mem_k, cache_mem_v, l, db, t),
            conv_prev=(p1, p2), q_dtype=F32)
        pool_s = jnp.concatenate([sp, pz_s.reshape(db, t, POOL_W)], axis=1)[:, -n_state:]
        conv_s = jnp.concatenate([sc, g_s.reshape(db, t, dff)], axis=1)[:, -(CONV_W - 1):]

        vals = [k_p.reshape(b, s, N_HEADS, V_DIM), v_p.reshape(b, s, N_HEADS, V_DIM),
                mk32.reshape(b, n_mem, X_HEADS, X_HEAD_DIM), mv32.reshape(b, n_mem, X_HEADS, X_HEAD_DIM),
                pz_p.reshape(b, s, POOL_W)[:, -n_state:], conv_p,
                k_s.reshape(db, t, N_HEADS, V_DIM), v_s.reshape(db, t, N_HEADS, V_DIM), pool_s, conv_s]
        for o, v in zip(outs, vals):
            o.append(v)
    return (yp.reshape(b, s, d), ys.reshape(db, t, d)) + tuple(jnp.stack(o) for o in outs)
```

```python
import functools
import math

import jax
import jax.numpy as jnp
from jax import lax
from jax.experimental import pallas as pl
from jax.experimental.pallas import tpu as pltpu

F32 = jnp.float32
BF16 = jnp.bfloat16

N_HEADS = 8
HEAD_DIM = 128
V_DIM = 2 * HEAD_DIM
ATTN_W = N_HEADS * V_DIM
POOL_WINDOWS = (2, 4, 8, 16)
POOL_GC = 256
POOL_W = len(POOL_WINDOWS) * POOL_GC
X_HEADS = 4
X_HEAD_DIM = 256
X_W = X_HEADS * X_HEAD_DIM
CONV_W = 3
ROPE_THETA = 10000.0
EPS = 1e-6
NEG = -1e30
LOG2_E = math.log2(math.e)

SUBLANES = 8
BF16_ROWS = 16
LANES = 128
PAGES_PER_STEP = 8
XATTN_GROUP = 4
MIB = 2 ** 20
STREAM_VMEM_MIB = 56


def _params(sem, vmem_mib):
    return pltpu.CompilerParams(dimension_semantics=sem, vmem_limit_bytes=vmem_mib * MIB)


def _row_tile(m, cap):
    t = min(m, cap)
    assert m % t == 0, (m, t)
    return t


def _dot(a, b):
    return jnp.dot(a, b, preferred_element_type=F32)


def _dot_nt(a, b):
    return lax.dot_general(a, b, (((1,), (1,)), ((), ())), preferred_element_type=F32)


def _rmsnorm_body(x_ref, w_ref, o_ref):
    x = x_ref[...]
    ms = jnp.mean(x * x, axis=-1, keepdims=True)
    o_ref[...] = (x * lax.rsqrt(ms + EPS) * w_ref[...]).astype(o_ref.dtype)


def _rmsnorm(x, w, name):
    m, d = x.shape
    tm = _row_tile(m, 256)
    return pl.pallas_call(
        _rmsnorm_body,
        out_shape=jax.ShapeDtypeStruct((m, d), BF16),
        grid=(m // tm,),
        in_specs=[pl.BlockSpec((tm, d), lambda i: (i, 0)), pl.BlockSpec((1, d), lambda i: (0, 0))],
        out_specs=pl.BlockSpec((tm, d), lambda i: (i, 0)),
        compiler_params=_params(("parallel",), 32),
        name=name,
    )(x, w.reshape(1, d))


def _tile_maps(ni, nj, lead, lag):
    def index(s):
        t = jnp.clip(s - lead - lag, 0, ni * nj - 1)
        return t % ni, t // ni

    return index


def _cast_weight_chunk(s, ni, w_ref, wbf):
    kc = w_ref.shape[0]
    slot = (s // ni) % 2
    r0 = pl.multiple_of((s % ni) * kc, kc)
    wbf[slot, pl.ds(r0, kc), :] = w_ref[...].astype(BF16)
    return 1 - slot


def _weight_specs(w, k, tn, ni, nj, cb0):
    if w.dtype == BF16:
        return pl.BlockSpec((k, tn), lambda s: (0, cb0 + jnp.minimum(s // ni, nj - 1))), [], 0
    assert k % (ni * BF16_ROWS) == 0
    spec = pl.BlockSpec((k // ni, tn), lambda s: (s % ni, cb0 + jnp.minimum(s // ni, nj - 1)))
    return spec, [pltpu.VMEM((2, k, tn), BF16)], ni


def _emit_spec(k, tn, ni, nj):
    return pl.BlockSpec((k, tn), lambda s: (0, jnp.clip(s // ni - 1, 0, nj - 1)))


def _proj_body(*refs, ni, nj, tn, chunk, rotary, sigmoid, n_out, delayed, cast, emit):
    x_ref, w_ref = refs[0], refs[1]
    idx = 2
    nw_ref = cos_ref = sin_ref = None
    if chunk:
        nw_ref = refs[idx]
        idx += 1
    if rotary:
        cos_ref, sin_ref = refs[idx], refs[idx + 1]
        idx += 2
    outs = refs[idx:idx + n_out]
    idx += n_out
    scratch = refs[idx + int(emit):]
    s = pl.program_id(0)
    if cast:
        wbf = scratch[0]
        scratch = scratch[1:]
        cur = _cast_weight_chunk(s, ni, w_ref, wbf)
        rhs = lambda: wbf[cur]
        t = s - ni
        if emit:
            @pl.when((s % ni == 0) & (s >= ni) & (s <= nj * ni))
            def _():
                refs[idx][...] = wbf[cur]
    else:
        rhs = lambda: w_ref[...]
        t = s

    def epilogue(z):
        if sigmoid:
            z = jax.nn.sigmoid(z)
        if not chunk:
            for o in outs:
                o[...] = z.astype(o.dtype)
            return
        nw = nw_ref[...]
        for c in range(tn // chunk):
            cols = slice(c * chunk, (c + 1) * chunk)
            zc = z[:, cols]
            ms = jnp.mean(zc * zc, axis=-1, keepdims=True)
            y = zc * lax.rsqrt(ms + EPS) * nw
            if rotary:
                y = y * cos_ref[...] + pltpu.roll(y, chunk // 2, 1) * sin_ref[...]
            for o in outs:
                o[:, cols] = y.astype(o.dtype)

    if not delayed:
        @pl.when(t >= 0)
        def _():
            epilogue(_dot(x_ref[...], rhs()))
        return

    za, zb = scratch

    @pl.when(t == 0)
    def _():
        zb[...] = jnp.zeros(zb.shape, F32)

    def step(z_w, z_r):
        z_w[...] = _dot(x_ref[...], rhs())
        epilogue(z_r[...])

    @pl.when((t >= 0) & (t % 2 == 0))
    def _():
        step(za, zb)

    @pl.when((t >= 0) & (t % 2 == 1))
    def _():
        step(zb, za)


def _proj(x, w, col0, ncols, out_dtypes, name, *, tm, norm_w=None, rope=None, rope_rep=1, sigmoid=False,
          emit=False, tn=512):
    m, k = x.shape
    assert m % tm == 0 and ncols % tn == 0 and col0 % tn == 0
    cb0 = col0 // tn
    ni, nj = m // tm, ncols // tn
    chunk = 0 if norm_w is None else norm_w.shape[-1]
    delayed = bool(chunk)
    w_spec, scratch, lead = _weight_specs(w, k, tn, ni, nj, cb0)
    assert not emit or lead
    mm = _tile_maps(ni, nj, lead, 0)
    ep = _tile_maps(ni, nj, lead, int(delayed))
    ins = [x, w]
    in_specs = [pl.BlockSpec((tm, k), lambda s: (mm(s)[0], 0)), w_spec]
    if chunk:
        ins.append(norm_w.reshape(1, chunk).astype(F32))
        in_specs.append(pl.BlockSpec((1, chunk), lambda s: (0, 0)))
    if rope is not None:
        for t in rope:
            ins.append(t)
            in_specs.append(pl.BlockSpec((tm, chunk), lambda s: (ep(s)[0] % rope_rep, 0)))
    out_shape = [jax.ShapeDtypeStruct((m, ncols), dt) for dt in out_dtypes]
    out_specs = [pl.BlockSpec((tm, tn), lambda s: ep(s)) for _ in out_dtypes]
    if emit:
        out_shape.append(jax.ShapeDtypeStruct((k, ncols), BF16))
        out_specs.append(_emit_spec(k, tn, ni, nj))
    body = functools.partial(_proj_body, ni=ni, nj=nj, tn=tn, chunk=chunk, rotary=rope is not None,
                             sigmoid=sigmoid, n_out=len(out_dtypes), delayed=delayed, cast=bool(lead), emit=emit)
    if delayed:
        scratch = scratch + [pltpu.VMEM((tm, tn), F32), pltpu.VMEM((tm, tn), F32)]
    res = pl.pallas_call(
        body,
        out_shape=out_shape,
        grid=(ni * nj + lead + int(delayed),),
        in_specs=in_specs,
        out_specs=out_specs,
        scratch_shapes=scratch,
        compiler_params=_params(("arbitrary",), STREAM_VMEM_MIB),
        name=name,
    )(*ins)
    return res


def _lam_value(lq1, lk1, lq2, lk2, lam_init):
    a = jnp.sum(lq1[...] * lk1[...], axis=-1, keepdims=True)
    b = jnp.sum(lq2[...] * lk2[...], axis=-1, keepdims=True)
    return jnp.exp(a) - jnp.exp(b) + lam_init


def _subln(o, sw, lam_init):
    ms = jnp.mean(o * o, axis=-1, keepdims=True)
    return (o * lax.rsqrt(ms + EPS) * sw) * (1.0 - lam_init)


def _attn_body(lq1, lk1, lq2, lk2, sw_ref, q_ref, k_ref, v_ref, o_ref,
               m1, l1, a1, m2, l2, a2, *, tq, lam_init, scale):
    qi = pl.program_id(2)
    q = q_ref[...]
    qs = (q[:, :HEAD_DIM], q[:, HEAD_DIM:])
    stats = ((m1, l1, a1), (m2, l2, a2))
    for m, l, a in stats:
        m[...] = jnp.full(m.shape, NEG, F32)
        l[...] = jnp.zeros(l.shape, F32)
        a[...] = jnp.zeros(a.shape, F32)

    def scores(j):
        start = pl.multiple_of(j * tq, tq)
        kb = k_ref[pl.ds(start, tq), :]
        return tuple(_dot_nt(qs[c], kb[:, c * HEAD_DIM:(c + 1) * HEAD_DIM]) for c in range(2))

    def accumulate(j, ss, mask):
        start = pl.multiple_of(j * tq, tq)
        vb = v_ref[pl.ds(start, tq), :]
        for s, (m, l, a) in zip(ss, stats):
            if mask is not None:
                s = jnp.where(mask, s, NEG)
            m_old = m[...]
            m_new = jnp.maximum(m_old, jnp.max(s, axis=-1, keepdims=True))
            alpha = jnp.exp2((m_old - m_new) * (scale * LOG2_E))
            p = jnp.exp2((s - m_new) * (scale * LOG2_E))
            l[...] = alpha * l[...] + jnp.sum(p, axis=-1, keepdims=True)
            a[...] = alpha * a[...] + _dot(p.astype(BF16), vb)
            m[...] = m_new

    def body(j, ss):
        nxt = scores(j + 1)
        accumulate(j, ss, None)
        return nxt

    ss = lax.fori_loop(0, qi, body, scores(0))
    row = lax.broadcasted_iota(jnp.int32, (tq, tq), 0)
    col = lax.broadcasted_iota(jnp.int32, (tq, tq), 1)
    accumulate(qi, ss, col <= row)

    lam = _lam_value(lq1, lk1, lq2, lk2, lam_init)
    o = a1[...] / l1[...] - lam * (a2[...] / l2[...])
    o_ref[...] = _subln(o, sw_ref[...], lam_init).astype(o_ref.dtype)


def _attn_prompt(q, k, v, lam_vecs, subln_w, b, s, lam_init):
    tq = _row_tile(s, 512)
    nq = s // tq
    vec = pl.BlockSpec((1, HEAD_DIM), lambda bi, h, i: (0, 0))
    body = functools.partial(_attn_body, tq=tq, lam_init=lam_init, scale=HEAD_DIM ** -0.5)
    return pl.pallas_call(
        body,
        out_shape=jax.ShapeDtypeStruct((b * s, ATTN_W), BF16),
        grid=(b, N_HEADS, nq),
        in_specs=[vec, vec, vec, vec,
                  pl.BlockSpec((1, V_DIM), lambda bi, h, i: (0, 0)),
                  pl.BlockSpec((tq, V_DIM), lambda bi, h, i: (bi * nq + i, h)),
                  pl.BlockSpec((s, V_DIM), lambda bi, h, i: (bi, h)),
                  pl.BlockSpec((s, V_DIM), lambda bi, h, i: (bi, h))],
        out_specs=pl.BlockSpec((tq, V_DIM), lambda bi, h, i: (bi * nq + i, h)),
        scratch_shapes=[pltpu.VMEM((tq, 1), F32), pltpu.VMEM((tq, 1), F32), pltpu.VMEM((tq, V_DIM), F32),
                        pltpu.VMEM((tq, 1), F32), pltpu.VMEM((tq, 1), F32), pltpu.VMEM((tq, V_DIM), F32)],
        compiler_params=_params(("parallel", "parallel", "arbitrary"), 40),
        name="attn_prompt",
    )(*lam_vecs, subln_w.reshape(1, V_DIM), q, k, v)


def _sattn_body(pt_ref, lq1, lk1, lq2, lk2, sw_ref, q_ref, kn_ref, vn_ref, *rest, npg, t_pad, lam_init, scale):
    del pt_ref
    k_refs = rest[:npg]
    v_refs = rest[npg:2 * npg]
    o_ref = rest[2 * npg]
    m_ref, l_ref, acc_ref = rest[2 * npg + 1:]
    c = pl.program_id(1)
    q = q_ref[...].astype(BF16)
    nrows = q.shape[0]

    def masks(ncols):
        row = lax.broadcasted_iota(jnp.int32, (nrows, ncols), 0)
        col = lax.broadcasted_iota(jnp.int32, (nrows, ncols), 1)
        same_head = (row // (2 * t_pad)) == (col % N_HEADS)
        return same_head, row % t_pad, col // N_HEADS

    def update(kbs, vbs, mask):
        ss = [jnp.where(mask, _dot_nt(q, kb) * scale, NEG) for kb in kbs]
        m_cur = functools.reduce(jnp.maximum, [jnp.max(s, axis=-1, keepdims=True) for s in ss])
        m_old = m_ref[...]
        m_new = jnp.maximum(m_old, m_cur)
        alpha = jnp.exp(m_old - m_new)
        ps = [jnp.exp(s - m_new) for s in ss]
        l_ref[...] = alpha * l_ref[...] + sum(jnp.sum(p, axis=-1, keepdims=True) for p in ps)
        acc_ref[...] = alpha * acc_ref[...] + sum(_dot(p.astype(BF16), vb) for p, vb in zip(ps, vbs))
        m_ref[...] = m_new

    @pl.when(c == 0)
    def _():
        m_ref[...] = jnp.full(m_ref.shape, NEG, F32)
        l_ref[...] = jnp.zeros(l_ref.shape, F32)
        acc_ref[...] = jnp.zeros(acc_ref.shape, F32)
        same_head, step, tok = masks(kn_ref.shape[0])
        update([kn_ref[...]], [vn_ref[...]], same_head & (tok <= step))

    update([r[...].astype(BF16) for r in k_refs], [r[...].astype(BF16) for r in v_refs],
           masks(k_refs[0].shape[0])[0])

    @pl.when(c == pl.num_programs(1) - 1)
    def _():
        lam = _lam_value(lq1, lk1, lq2, lk2, lam_init)
        y = acc_ref[...] / l_ref[...]
        o = y - lam * pltpu.roll(y, nrows - t_pad, 0)
        o_ref[...] = _subln(o, sw_ref[...], lam_init)


def _attn_sample(q32, k_bf, v_bf, cache_k, cache_v, layer, page_table, lam_vecs, subln_w, db, t, lam_init):
    t_pad = SUBLANES // 2
    assert t <= t_pad
    depth, n_phys, page, _, _ = cache_k.shape
    n_pages = page_table.shape[1]
    npg = math.gcd(PAGES_PER_STEP, n_pages)
    ck = cache_k.reshape(depth, n_phys, page * N_HEADS, V_DIM)
    cv = cache_v.reshape(depth, n_phys, page * N_HEADS, V_DIM)
    q5 = q32.reshape(db, t, N_HEADS, 2, HEAD_DIM).transpose(0, 2, 3, 1, 4)
    q5 = jnp.pad(q5, ((0, 0), (0, 0), (0, 0), (0, t_pad - t), (0, 0)))
    eye = jnp.eye(2, dtype=F32)
    nrows = N_HEADS * 2 * t_pad
    qbd = (q5[:, :, :, :, None, :] * eye[None, None, :, None, :, None]).reshape(db, nrows, V_DIM)
    new_tok = LANES // N_HEADS
    pad = ((0, 0), (0, new_tok - t), (0, 0), (0, 0))
    kn = jnp.pad(k_bf.reshape(db, t, N_HEADS, V_DIM), pad).reshape(db, LANES, V_DIM)
    vn = jnp.pad(v_bf.reshape(db, t, N_HEADS, V_DIM), pad).reshape(db, LANES, V_DIM)

    def page_spec(i):
        return pl.BlockSpec((None, None, page * N_HEADS, V_DIM),
                            lambda b, c, pt: (layer, pt[b, c * npg + i], 0, 0))

    vec = pl.BlockSpec((1, HEAD_DIM), lambda b, c, pt: (0, 0))
    per_seq = lambda rows: pl.BlockSpec((None, rows, V_DIM), lambda b, c, pt: (b, 0, 0))
    in_specs = [vec, vec, vec, vec, pl.BlockSpec((1, V_DIM), lambda b, c, pt: (0, 0)),
                per_seq(nrows), per_seq(LANES), per_seq(LANES)]
    in_specs += [page_spec(i) for i in range(npg)] * 2
    body = functools.partial(_sattn_body, npg=npg, t_pad=t_pad, lam_init=lam_init, scale=HEAD_DIM ** -0.5)
    out = pl.pallas_call(
        body,
        out_shape=jax.ShapeDtypeStruct((db, nrows, V_DIM), F32),
        grid_spec=pltpu.PrefetchScalarGridSpec(
            num_scalar_prefetch=1,
            grid=(db, n_pages // npg),
            in_specs=in_specs,
            out_specs=per_seq(nrows),
            scratch_shapes=[pltpu.VMEM((nrows, 1), F32), pltpu.VMEM((nrows, 1), F32),
                            pltpu.VMEM((nrows, V_DIM), F32)]),
        compiler_params=_params(("arbitrary", "arbitrary"), 48),
        name="attn_sample",
    )(page_table, *lam_vecs, subln_w.reshape(1, V_DIM), qbd, kn, vn, *([ck] * npg), *([cv] * npg))
    out = out.reshape(db, N_HEADS, 2, t_pad, V_DIM)[:, :, 0, :t]
    return out.transpose(0, 2, 1, 3).reshape(db * t, ATTN_W)


def _xattn_prompt_body(q_ref, k_ref, v_ref, o_ref, *, scale):
    s = _dot_nt(q_ref[...], k_ref[...]) * scale
    m = jnp.max(s, axis=-1, keepdims=True)
    p = jnp.exp(s - m)
    p = p / jnp.sum(p, axis=-1, keepdims=True)
    o_ref[...] = _dot(p.astype(BF16), v_ref[...]).astype(o_ref.dtype)


def _xattn_prompt(xq, mk, mv, b, s, n_mem):
    tq = _row_tile(s, 1024)
    nq = s // tq
    body = functools.partial(_xattn_prompt_body, scale=X_HEAD_DIM ** -0.5)
    return pl.pallas_call(
        body,
        out_shape=jax.ShapeDtypeStruct((b * s, X_W), BF16),
        grid=(b, X_HEADS, nq),
        in_specs=[pl.BlockSpec((tq, X_HEAD_DIM), lambda bi, h, i: (bi * nq + i, h)),
                  pl.BlockSpec((n_mem, X_HEAD_DIM), lambda bi, h, i: (bi, h)),
                  pl.BlockSpec((n_mem, X_HEAD_DIM), lambda bi, h, i: (bi, h))],
        out_specs=pl.BlockSpec((tq, X_HEAD_DIM), lambda bi, h, i: (bi * nq + i, h)),
        compiler_params=_params(("parallel", "parallel", "parallel"), 32),
        name="xattn_prompt",
    )(xq, mk, mv)


def _xattn_sample_body(q_ref, k_ref, v_ref, o_ref, *, t, scale):
    g, n_mem = k_ref.shape[0], k_ref.shape[1]
    rows = q_ref.shape[0]
    ncols = g * n_mem * X_HEADS
    kb = k_ref[...].reshape(ncols, X_HEAD_DIM).astype(BF16)
    vb = v_ref[...].reshape(ncols, X_HEAD_DIM).astype(BF16)
    row = lax.broadcasted_iota(jnp.int32, (rows, ncols), 0)
    col = lax.broadcasted_iota(jnp.int32, (rows, ncols), 1)
    same = ((row // (t * X_HEADS)) == (col // (n_mem * X_HEADS))) & ((row % X_HEADS) == (col % X_HEADS))
    s = jnp.where(same, _dot_nt(q_ref[...], kb) * scale, NEG)
    m = jnp.max(s, axis=-1, keepdims=True)
    p = jnp.exp(s - m)
    p = p / jnp.sum(p, axis=-1, keepdims=True)
    o_ref[...] = _dot(p.astype(BF16), vb).astype(o_ref.dtype)


def _xattn_sample(xq, mem_k, mem_v, layer, db, t):
    n_mem = mem_k.shape[2]
    g = math.gcd(XATTN_GROUP, db)
    rows = g * t * X_HEADS
    q = xq.reshape(db * t * X_HEADS, X_HEAD_DIM)
    mem_spec = pl.BlockSpec((None, g, n_mem, X_HEADS, X_HEAD_DIM), lambda i: (layer, i, 0, 0, 0))
    body = functools.partial(_xattn_sample_body, t=t, scale=X_HEAD_DIM ** -0.5)
    out = pl.pallas_call(
        body,
        out_shape=jax.ShapeDtypeStruct((db * t * X_HEADS, X_HEAD_DIM), BF16),
        grid=(db // g,),
        in_specs=[pl.BlockSpec((rows, X_HEAD_DIM), lambda i: (i, 0)), mem_spec, mem_spec],
        out_specs=pl.BlockSpec((rows, X_HEAD_DIM), lambda i: (i, 0)),
        compiler_params=_params(("parallel",), 40),
        name="xattn_sample",
    )(q, mem_k, mem_v)
    return out.reshape(db * t, X_W)


POOL_HALO = 16


def _pool_prompt_body(x_ref, halo_ref, pw_ref, ps_ref, o_ref, buf, *, tm, tiles_per_seq):
    i = pl.program_id(0)
    first = (i % tiles_per_seq) == 0
    x = x_ref[...]
    buf[0:POOL_HALO, :] = jnp.where(first, 0.0, halo_ref[...])
    buf[POOL_HALO:, :] = x
    pos = (i % tiles_per_seq) * tm + lax.broadcasted_iota(jnp.int32, (tm, 1), 0)
    for g, w in enumerate(POOL_WINDOWS):
        cols = slice(g * POOL_GC, (g + 1) * POOL_GC)
        xs = x[:, cols]
        acc = xs
        for j in range(1, w):
            acc = acc + buf[POOL_HALO - j:POOL_HALO - j + tm, cols]
        cnt = jnp.minimum(w, pos + 1).astype(F32)
        d = acc / cnt - xs
        o_ref[:, cols] = (_dot(d.astype(BF16), pw_ref[g]) * ps_ref[:, cols]).astype(o_ref.dtype)


def _pool_prompt(pz, pool_w, pool_scale, s):
    m = pz.shape[0]
    tm = _row_tile(s, 512)
    hb = tm // POOL_HALO
    body = functools.partial(_pool_prompt_body, tm=tm, tiles_per_seq=s // tm)
    return pl.pallas_call(
        body,
        out_shape=jax.ShapeDtypeStruct((m, POOL_W), BF16),
        grid=(m // tm,),
        in_specs=[pl.BlockSpec((tm, POOL_W), lambda i: (i, 0)),
                  pl.BlockSpec((POOL_HALO, POOL_W), lambda i: (jnp.maximum(i * hb - 1, 0), 0)),
                  pl.BlockSpec(pool_w.shape, lambda i: (0, 0, 0)),
                  pl.BlockSpec((1, POOL_W), lambda i: (0, 0))],
        out_specs=pl.BlockSpec((tm, POOL_W), lambda i: (i, 0)),
        scratch_shapes=[pltpu.VMEM((tm + POOL_HALO, POOL_W), F32)],
        compiler_params=_params(("parallel",), 32),
        name="pool_prompt",
    )(pz, pz, pool_w, pool_scale.reshape(1, POOL_W))


def _pool_sample_body(st_ref, pz_ref, pw_ref, ps_ref, o_ref, *, n_state, t, past_len):
    def row(idx, cols):
        return st_ref[idx][:, cols] if idx < n_state else pz_ref[idx - n_state][:, cols]

    for g, w in enumerate(POOL_WINDOWS):
        cols = slice(g * POOL_GC, (g + 1) * POOL_GC)
        ds = []
        for ti in range(t):
            hi = n_state + ti + 1
            lo = max(hi - w, 0)
            acc = row(hi - 1, cols)
            for idx in range(hi - 2, lo - 1, -1):
                acc = acc + row(idx, cols)
            cnt = float(min(w, past_len + ti + 1))
            ds.append(acc / cnt - row(hi - 1, cols))
        d = jnp.concatenate(ds, axis=0)
        o_ref[:, cols] = (_dot(d.astype(BF16), pw_ref[g]) * ps_ref[:, cols]).astype(o_ref.dtype)


def _pool_sample(pz, state_pool, pool_w, pool_scale, db, t, past_len):
    n_state = state_pool.shape[1]
    st = state_pool.transpose(1, 0, 2)
    pzt = pz.reshape(db, t, POOL_W).transpose(1, 0, 2)
    body = functools.partial(_pool_sample_body, n_state=n_state, t=t, past_len=past_len)
    out = pl.pallas_call(
        body,
        out_shape=jax.ShapeDtypeStruct((t * db, POOL_W), BF16),
        grid=(1,),
        in_specs=[pl.BlockSpec(st.shape, lambda i: (0, 0, 0)),
                  pl.BlockSpec(pzt.shape, lambda i: (0, 0, 0)),
                  pl.BlockSpec(pool_w.shape, lambda i: (0, 0, 0)),
                  pl.BlockSpec((1, POOL_W), lambda i: (0, 0))],
        out_specs=pl.BlockSpec((t * db, POOL_W), lambda i: (0, 0)),
        compiler_params=_params(("arbitrary",), 40),
        name="pool_sample",
    )(st, pzt, pool_w, pool_scale.reshape(1, POOL_W))
    return out.reshape(t, db, POOL_W).transpose(1, 0, 2).reshape(db * t, POOL_W)


def _merge_body(a_ref, p_ref, c_ref, wa_ref, wp_ref, wc_ref, g0_ref, g1_ref, g2_ref, o_ref):
    m = (g0_ref[...] * _dot(a_ref[...], wa_ref[...])
         + g1_ref[...] * _dot(p_ref[...], wp_ref[...])
         + g2_ref[...] * _dot(c_ref[...], wc_ref[...]))
    o_ref[...] = m.astype(o_ref.dtype)


def _merge(a, po, c, gates, wa, wp, wc, d, tm):
    m = a.shape[0]
    assert m % tm == 0
    tn = 512
    nj = d // tn
    row = lambda k: pl.BlockSpec((tm, k), lambda i, j: (i, 0))
    wcol = lambda k: pl.BlockSpec((k, tn), lambda i, j: (0, j))
    gate = lambda r: pl.BlockSpec((tm, tn), lambda i, j: (i, r * nj + j))
    return pl.pallas_call(
        _merge_body,
        out_shape=jax.ShapeDtypeStruct((m, d), BF16),
        grid=(m // tm, nj),
        in_specs=[row(ATTN_W), row(POOL_W), row(X_W), wcol(ATTN_W), wcol(POOL_W), wcol(X_W),
                  gate(0), gate(1), gate(2)],
        out_specs=pl.BlockSpec((tm, tn), lambda i, j: (i, j)),
        compiler_params=_params(("parallel", "arbitrary"), 48),
        name="merge",
    )(a, po, c, wa, wp, wc, gates, gates, gates)


def _resid_body(x_ref, w_ref, r_ref, o_ref):
    o_ref[...] = r_ref[...] + _dot(x_ref[...], w_ref[...])


def _resid_proj(x, w, resid, name, *, tm, tn):
    m, k = x.shape
    n = w.shape[1]
    assert m % tm == 0 and n % tn == 0
    return pl.pallas_call(
        _resid_body,
        out_shape=jax.ShapeDtypeStruct((m, n), F32),
        grid=(m // tm, n // tn),
        in_specs=[pl.BlockSpec((tm, k), lambda i, j: (i, 0)),
                  pl.BlockSpec((k, tn), lambda i, j: (0, j)),
                  pl.BlockSpec((tm, tn), lambda i, j: (i, j))],
        out_specs=pl.BlockSpec((tm, tn), lambda i, j: (i, j)),
        compiler_params=_params(("parallel", "arbitrary"), STREAM_VMEM_MIB),
        name=name,
    )(x, w, resid)


CONV_HALO = SUBLANES


def _ffn_a_body(*refs, ni, nj, tm, tiles_per_seq, seq_len, has_prev, cast, emit):
    h_ref, wg_ref, wu_ref, cw_ref, cb_ref = refs[:5]
    idx = 5
    if has_prev:
        p1_ref, p2_ref = refs[idx:idx + 2]
        idx += 2
    act_ref, g_ref = refs[idx:idx + 2]
    idx += 2
    scratch = refs[idx + 2 * int(emit):]
    s = pl.program_id(0)
    if cast:
        wgb, wub, buf, carry = scratch
        cur = _cast_weight_chunk(s, ni, wg_ref, wgb)
        _cast_weight_chunk(s, ni, wu_ref, wub)
        rhs = (lambda: wgb[cur]), (lambda: wub[cur])
        t = s - ni
        if emit:
            @pl.when((s % ni == 0) & (s >= ni) & (s <= nj * ni))
            def _():
                refs[idx][...] = wgb[cur]
                refs[idx + 1][...] = wub[cur]
    else:
        buf, carry = scratch
        rhs = (lambda: wg_ref[...]), (lambda: wu_ref[...])
        t = s

    @pl.when(s == 0)
    def _():
        carry[...] = jnp.zeros(carry.shape, F32)

    @pl.when(t >= 0)
    def _():
        _ffn_a_tile(s, rhs, h_ref, cw_ref, cb_ref, p1_ref if has_prev else None,
                    p2_ref if has_prev else None, act_ref, g_ref, buf, carry, ni=ni, tm=tm,
                    tiles_per_seq=tiles_per_seq, seq_len=seq_len)


def _ffn_a_tile(s, rhs, h_ref, cw_ref, cb_ref, p1_ref, p2_ref, act_ref, g_ref, buf, carry, *,
                ni, tm, tiles_per_seq, seq_len):
    has_prev = p1_ref is not None
    h = h_ref[...]
    g = _dot(h, rhs[0]())
    u = _dot(h, rhs[1]())
    if tiles_per_seq:
        first = ((s % ni) % tiles_per_seq) == 0
        buf[0:CONV_HALO, :] = jnp.where(first, 0.0, carry[...])
    else:
        buf[0:CONV_HALO, :] = jnp.zeros((CONV_HALO, g.shape[1]), F32)
    buf[CONV_HALO:, :] = g
    carry[...] = g[tm - CONV_HALO:, :]
    g1 = buf[CONV_HALO - 1:CONV_HALO - 1 + tm, :]
    g2 = buf[CONV_HALO - 2:CONV_HALO - 2 + tm, :]
    if not tiles_per_seq:
        t = lax.broadcasted_iota(jnp.int32, (tm, 1), 0) % seq_len
        g1 = jnp.where(t >= 1, g1, 0.0)
        g2 = jnp.where(t >= 2, g2, 0.0)
    if has_prev:
        g1 = g1 + p1_ref[...]
        g2 = g2 + p2_ref[...]
    cw = cw_ref[...]
    gc = cb_ref[...] + g2 * cw[0:1, :]
    gc = gc + g1 * cw[1:2, :]
    gc = gc + g * cw[2:3, :]
    act_ref[...] = (jax.nn.silu(gc) * u).astype(act_ref.dtype)
    if tiles_per_seq:
        g_ref[...] = g[tm - CONV_HALO:, :]
    else:
        g_ref[...] = g


def _ffn_a(h2, wg, wu, conv_w, conv_b, seq_len, tm, prev=None, emit=False):
    m, d = h2.shape
    dff = wg.shape[1]
    tn = 256
    assert dff % tn == 0 and m % tm == 0
    long_seq = seq_len >= tm
    if long_seq:
        assert seq_len % tm == 0 and prev is None
        tiles_per_seq = seq_len // tm
    else:
        assert tm == m and tm % seq_len == 0
        tiles_per_seq = 0
    ni, nj = m // tm, dff // tn
    w_spec, w_scratch, lead = _weight_specs(wg, d, tn, ni, nj, 0)
    assert wu.dtype == wg.dtype and (not emit or lead)
    tile = _tile_maps(ni, nj, lead, 0)
    col = lambda s: (0, tile(s)[1])
    ins = [h2, wg, wu, conv_w, conv_b.reshape(1, dff)]
    in_specs = [pl.BlockSpec((tm, d), lambda s: (tile(s)[0], 0)),
                w_spec,
                w_spec,
                pl.BlockSpec((CONV_W, tn), col),
                pl.BlockSpec((1, tn), col)]
    if prev is not None:
        ins += list(prev)
        in_specs += [pl.BlockSpec((tm, tn), tile)] * 2
    if long_seq:
        g_shape = jax.ShapeDtypeStruct((ni, CONV_HALO, dff), F32)
        g_spec = pl.BlockSpec((None, CONV_HALO, tn), lambda s: (tile(s)[0], 0, tile(s)[1]))
    else:
        g_shape = jax.ShapeDtypeStruct((m, dff), F32)
        g_spec = pl.BlockSpec((tm, tn), tile)
    body = functools.partial(_ffn_a_body, ni=ni, nj=nj, tm=tm, tiles_per_seq=tiles_per_seq, seq_len=seq_len,
                             has_prev=prev is not None, cast=bool(lead), emit=emit)
    out_shape = [jax.ShapeDtypeStruct((m, dff), BF16), g_shape]
    out_specs = [pl.BlockSpec((tm, tn), tile), g_spec]
    if emit:
        out_shape += [jax.ShapeDtypeStruct((d, dff), BF16)] * 2
        out_specs += [_emit_spec(d, tn, ni, nj)] * 2
    return pl.pallas_call(
        body,
        out_shape=out_shape,
        grid=(ni * nj + lead,),
        in_specs=in_specs,
        out_specs=out_specs,
        scratch_shapes=w_scratch + w_scratch + [pltpu.VMEM((tm + CONV_HALO, tn), F32),
                                                pltpu.VMEM((CONV_HALO, tn), F32)],
        compiler_params=_params(("arbitrary",), STREAM_VMEM_MIB),
        name="ffn_gate_up",
    )(*ins)


def _rope_tables(pos):
    half = HEAD_DIM // 2
    inv = ROPE_THETA ** (-jnp.arange(half, dtype=F32) / half)
    ang = pos.astype(F32)[:, None] * inv[None, :]
    cos, sin = jnp.cos(ang), jnp.sin(ang)
    return jnp.concatenate([cos, cos], axis=-1), jnp.concatenate([-sin, sin], axis=-1)


def _layer(x2, w, lam_init, *, tm, seq_len, rope, attn_fn, pool_fn, xattn_fn, conv_prev, q_dtype, wbf=None):
    d = x2.shape[1]
    rope_rep = rope[0].shape[0] // tm
    pool_off = 3 * ATTN_W
    xq_off = pool_off + POOL_W
    gate_off = xq_off + X_W
    emit = wbf is None
    new_wbf = {}

    def in_proj(key, col0, ncols, out_dtypes, **kw):
        if emit:
            *res, new_wbf[key] = _proj(h, w['w_in'], col0, ncols, out_dtypes, "proj_" + key, tm=tm, emit=True,
                                       **kw)
            return res
        return _proj(h, wbf[key], 0, ncols, out_dtypes, "proj_" + key, tm=tm, **kw)

    h = _rmsnorm(x2, w['attn_norm_w'], "attn_norm")
    (q,) = in_proj('q', 0, ATTN_W, [q_dtype], norm_w=w['q_norm_w'], rope=rope, rope_rep=rope_rep)
    k32, kbf = in_proj('k', ATTN_W, ATTN_W, [F32, BF16], norm_w=w['k_norm_w'], rope=rope, rope_rep=rope_rep)
    v32, vbf = in_proj('v', 2 * ATTN_W, ATTN_W, [F32, BF16])
    (pz,) = in_proj('pool', pool_off, POOL_W, [F32])
    (xq,) = in_proj('xq', xq_off, X_W, [BF16], norm_w=w['xq_norm_w'])
    (gates,) = in_proj('gates', gate_off, 3 * d, [F32], sigmoid=True)
    a = attn_fn(q, kbf, vbf)
    po = pool_fn(pz)
    c = xattn_fn(xq)
    m = _merge(a, po, c, gates, w['w_branch_attn'], w['w_branch_pool'], w['w_branch_cross'], d, tm)
    x1 = _resid_proj(m, w['w_out'], x2, "out_proj", tm=tm, tn=512)
    h2 = _rmsnorm(x1, w['ffn_norm_w'], "ffn_norm")
    if emit:
        act, gl, new_wbf['gate'], new_wbf['up'] = _ffn_a(h2, w['w_gate'], w['w_up'], w['conv_w'], w['conv_b'],
                                                         seq_len, tm, conv_prev, emit=True)
    else:
        act, gl = _ffn_a(h2, wbf['gate'], wbf['up'], w['conv_w'], w['conv_b'], seq_len, tm, conv_prev)
    y = _resid_proj(act, w['w_down'], x1, "down_proj", tm=min(tm, 512), tn=512)
    return y, k32, v32, pz, gl, new_wbf


def kernel(x_prompt, x_sample, cache_k, cache_v, cache_mem_k, cache_mem_v, state_pool, state_conv, page_table, mem_prompt, attn_norm_w, w_in, q_norm_w, k_norm_w, lambda_q1, lambda_k1, lambda_q2, lambda_k2, subln_w, pool_w, pool_scale, mem_norm_w, w_mem_k, w_mem_v, xq_norm_w, xk_norm_w, w_branch_attn, w_branch_pool, w_branch_cross, w_out, ffn_norm_w, w_gate, w_up, conv_w, conv_b, w_down):
    b, s, d = x_prompt.shape
    db, t, _ = x_sample.shape
    depth = w_in.shape[0]
    dff = w_gate.shape[-1]
    n_mem = mem_prompt.shape[1]
    past_len = page_table.shape[1] * cache_k.shape[2]
    n_state = state_pool.shape[2]
    rope_p = _rope_tables(jnp.arange(s, dtype=jnp.int32))
    rope_s = _rope_tables(jnp.tile(past_len + jnp.arange(t, dtype=jnp.int32), db))
    yp = x_prompt.reshape(b * s, d)
    ys = x_sample.reshape(db * t, d)
    outs = [[] for _ in range(10)]
    for l in range(depth):
        lam_init = 0.8 - 0.6 * math.exp(-0.3 * l)
        w = {'attn_norm_w': attn_norm_w[l], 'w_in': w_in[l], 'q_norm_w': q_norm_w[l],
             'k_norm_w': k_norm_w[l], 'xq_norm_w': xq_norm_w[l],
             'w_branch_attn': w_branch_attn[l].astype(BF16), 'w_branch_pool': w_branch_pool[l].astype(BF16),
             'w_branch_cross': w_branch_cross[l].astype(BF16), 'w_out': w_out[l].astype(BF16),
             'ffn_norm_w': ffn_norm_w[l], 'w_gate': w_gate[l], 'w_up': w_up[l],
             'conv_w': conv_w[l], 'conv_b': conv_b[l], 'w_down': w_down[l].astype(BF16)}
        lam_vecs = [v[l].reshape(1, HEAD_DIM).astype(F32) for v in (lambda_q1, lambda_k1, lambda_q2, lambda_k2)]
        pw = pool_w[l].astype(BF16)
        ps = pool_scale[l]
        sw = subln_w[l]

        mn = _rmsnorm(mem_prompt.reshape(b * n_mem, d), mem_norm_w[l], "mem_norm")
        tm_m = _row_tile(b * n_mem, 1024)
        mk32, mkbf = _proj(mn, w_mem_k[l], 0, X_W, [F32, BF16], "proj_mem_k", tm=tm_m,
                           norm_w=xk_norm_w[l])
        mv32, mvbf = _proj(mn, w_mem_v[l], 0, X_W, [F32, BF16], "proj_mem_v", tm=tm_m)
        tm_p = _row_tile(s, 1024)
        yp, k_p, v_p, pz_p, gl_p, wbf = _layer(
            yp, w, lam_init, tm=tm_p, seq_len=s, rope=rope_p,
            attn_fn=lambda q, k, v: _attn_prompt(q, k, v, lam_vecs, sw, b, s, lam_init),
            pool_fn=lambda pz: _pool_prompt(pz, pw, ps, s),
            xattn_fn=lambda xq: _xattn_prompt(xq, mkbf, mvbf, b, s, n_mem),
            conv_prev=None, q_dtype=BF16)
        tiles_per_seq = s // tm_p
        conv_p = gl_p.reshape(b, tiles_per_seq, CONV_HALO, dff)[:, -1, CONV_HALO - (CONV_W - 1):]

        sc = state_conv[l]
        zrow = jnp.zeros((db, 1, dff), F32)
        p1 = jnp.concatenate([sc[:, 1:2]] + [zrow] * (t - 1), axis=1).reshape(db * t, dff)
        p2 = jnp.concatenate([sc[:, 0:1], sc[:, 1:2]] + [zrow] * (t - 2), axis=1).reshape(db * t, dff)
        sp = state_pool[l]
        ys, k_s, v_s, pz_s, g_s, _ = _layer(
            ys, w, lam_init, tm=_row_tile(db * t, 1024), seq_len=t, rope=rope_s,
            attn_fn=lambda q, k, v: _attn_sample(q, k, v, cache_k, cache_v, l, page_table, lam_vecs, sw, db, t,
                                                 lam_init).astype(BF16),
            pool_fn=lambda pz: _pool_sample(pz, sp, pw, ps, db, t, past_len),
            xattn_fn=lambda xq: _xattn_sample(xq, cache_mem_k, cache_mem_v, l, db, t),
            conv_prev=(p1, p2), q_dtype=F32, wbf=wbf)
        pool_s = jnp.concatenate([sp, pz_s.reshape(db, t, POOL_W)], axis=1)[:, -n_state:]
        conv_s = jnp.concatenate([sc, g_s.reshape(db, t, dff)], axis=1)[:, -(CONV_W - 1):]

        vals = [k_p.reshape(b, s, N_HEADS, V_DIM), v_p.reshape(b, s, N_HEADS, V_DIM),
                mk32.reshape(b, n_mem, X_HEADS, X_HEAD_DIM), mv32.reshape(b, n_mem, X_HEADS, X_HEAD_DIM),
                pz_p.reshape(b, s, POOL_W)[:, -n_state:], conv_p,
                k_s.reshape(db, t, N_HEADS, V_DIM), v_s.reshape(db, t, N_HEADS, V_DIM), pool_s, conv_s]
        for o, v in zip(outs, vals):
            o.append(v)
    return (yp.reshape(b, s, d), ys.reshape(db, t, d)) + tuple(jnp.stack(o) for o in outs)
```

```python
import functools
import math

import jax
import jax.numpy as jnp
from jax import lax
from jax.experimental import pallas as pl
from jax.experimental.pallas import tpu as pltpu

F32 = jnp.float32
BF16 = jnp.bfloat16

N_HEADS = 8
HEAD_DIM = 128
V_DIM = 2 * HEAD_DIM
ATTN_W = N_HEADS * V_DIM
POOL_WINDOWS = (2, 4, 8, 16)
POOL_GC = 256
POOL_W = len(POOL_WINDOWS) * POOL_GC
X_HEADS = 4
X_HEAD_DIM = 256
X_W = X_HEADS * X_HEAD_DIM
CONV_W = 3
ROPE_THETA = 10000.0
EPS = 1e-6
NEG = -1e30
LOG2_E = math.log2(math.e)

SUBLANES = 8
BF16_ROWS = 16
LANES = 128
PAGES_PER_STEP = 8
XATTN_GROUP = 4
MIB = 2 ** 20
STREAM_VMEM_MIB = 56


def _params(sem, vmem_mib):
    return pltpu.CompilerParams(dimension_semantics=sem, vmem_limit_bytes=vmem_mib * MIB)


def _row_tile(m, cap):
    t = min(m, cap)
    assert m % t == 0, (m, t)
    return t


def _dot(a, b):
    return jnp.dot(a, b, preferred_element_type=F32)


def _dot_nt(a, b):
    return lax.dot_general(a, b, (((1,), (1,)), ((), ())), preferred_element_type=F32)


def _rmsnorm_body(x_ref, w_ref, o_ref):
    x = x_ref[...]
    ms = jnp.mean(x * x, axis=-1, keepdims=True)
    o_ref[...] = (x * lax.rsqrt(ms + EPS) * w_ref[...]).astype(o_ref.dtype)


def _rmsnorm(x, w, name):
    m, d = x.shape
    tm = _row_tile(m, 256)
    return pl.pallas_call(
        _rmsnorm_body,
        out_shape=jax.ShapeDtypeStruct((m, d), BF16),
        grid=(m // tm,),
        in_specs=[pl.BlockSpec((tm, d), lambda i: (i, 0)), pl.BlockSpec((1, d), lambda i: (0, 0))],
        out_specs=pl.BlockSpec((tm, d), lambda i: (i, 0)),
        compiler_params=_params(("parallel",), 32),
        name=name,
    )(x, w.reshape(1, d))


def _tile_maps(ni, nj, lead, lag):
    def index(s):
        t = jnp.clip(s - lead - lag, 0, ni * nj - 1)
        return t % ni, t // ni

    return index


def _cast_weight_chunk(s, ni, w_ref, wbf):
    kc = w_ref.shape[0]
    slot = (s // ni) % 2
    r0 = pl.multiple_of((s % ni) * kc, kc)
    wbf[slot, pl.ds(r0, kc), :] = w_ref[...].astype(BF16)
    return 1 - slot


def _weight_specs(w, k, tn, ni, nj, cb0):
    if w.dtype == BF16:
        return pl.BlockSpec((k, tn), lambda s: (0, cb0 + jnp.minimum(s // ni, nj - 1))), [], 0
    assert k % (ni * BF16_ROWS) == 0
    spec = pl.BlockSpec((k // ni, tn), lambda s: (s % ni, cb0 + jnp.minimum(s // ni, nj - 1)))
    return spec, [pltpu.VMEM((2, k, tn), BF16)], ni


def _emit_spec(k, tn, ni, nj):
    return pl.BlockSpec((k, tn), lambda s: (0, jnp.clip(s // ni - 1, 0, nj - 1)))


def _side_cast_specs(arrays, n_steps):
    in_specs, out_specs, out_shape = [], [], []
    for a in arrays:
        rows, cols = a.shape
        rb = BF16_ROWS
        while rows % rb or rows // rb > n_steps:
            rb += BF16_ROWS
        index = lambda s, last=rows // rb - 1: (jnp.minimum(s, last), 0)
        in_specs.append(pl.BlockSpec((rb, cols), index))
        out_specs.append(pl.BlockSpec((rb, cols), index))
        out_shape.append(jax.ShapeDtypeStruct((rows, cols), BF16))
    return in_specs, out_specs, out_shape


def _proj_body(*refs, ni, nj, tn, chunk, rotary, sigmoid, n_out, delayed, cast, emit, n_side):
    x_ref, w_ref = refs[0], refs[1]
    idx = 2
    nw_ref = cos_ref = sin_ref = None
    if chunk:
        nw_ref = refs[idx]
        idx += 1
    if rotary:
        cos_ref, sin_ref = refs[idx], refs[idx + 1]
        idx += 2
    side_in = refs[idx:idx + n_side]
    idx += n_side
    outs = refs[idx:idx + n_out]
    idx += n_out
    side_out = refs[idx + int(emit):idx + int(emit) + n_side]
    scratch = refs[idx + int(emit) + n_side:]
    s = pl.program_id(0)
    for src, dst in zip(side_in, side_out):
        dst[...] = src[...].astype(BF16)
    if cast:
        wbf = scratch[0]
        scratch = scratch[1:]
        cur = _cast_weight_chunk(s, ni, w_ref, wbf)
        rhs = lambda: wbf[cur]
        t = s - ni
        if emit:
            @pl.when((s % ni == 0) & (s >= ni) & (s <= nj * ni))
            def _():
                refs[idx][...] = wbf[cur]
    else:
        rhs = lambda: w_ref[...]
        t = s

    def epilogue(z):
        if sigmoid:
            z = jax.nn.sigmoid(z)
        if not chunk:
            for o in outs:
                o[...] = z.astype(o.dtype)
            return
        nw = nw_ref[...]
        for c in range(tn // chunk):
            cols = slice(c * chunk, (c + 1) * chunk)
            zc = z[:, cols]
            ms = jnp.mean(zc * zc, axis=-1, keepdims=True)
            y = zc * lax.rsqrt(ms + EPS) * nw
            if rotary:
                y = y * cos_ref[...] + pltpu.roll(y, chunk // 2, 1) * sin_ref[...]
            for o in outs:
                o[:, cols] = y.astype(o.dtype)

    if not delayed:
        @pl.when(t >= 0)
        def _():
            epilogue(_dot(x_ref[...], rhs()))
        return

    za, zb = scratch

    @pl.when(t == 0)
    def _():
        zb[...] = jnp.zeros(zb.shape, F32)

    def step(z_w, z_r):
        z_w[...] = _dot(x_ref[...], rhs())
        epilogue(z_r[...])

    @pl.when((t >= 0) & (t % 2 == 0))
    def _():
        step(za, zb)

    @pl.when((t >= 0) & (t % 2 == 1))
    def _():
        step(zb, za)


def _proj(x, w, col0, ncols, out_dtypes, name, *, tm, norm_w=None, rope=None, rope_rep=1, sigmoid=False,
          emit=False, side_casts=(), tn=512):
    m, k = x.shape
    assert m % tm == 0 and ncols % tn == 0 and col0 % tn == 0
    cb0 = col0 // tn
    ni, nj = m // tm, ncols // tn
    chunk = 0 if norm_w is None else norm_w.shape[-1]
    delayed = bool(chunk)
    w_spec, scratch, lead = _weight_specs(w, k, tn, ni, nj, cb0)
    assert not emit or lead
    mm = _tile_maps(ni, nj, lead, 0)
    ep = _tile_maps(ni, nj, lead, int(delayed))
    ins = [x, w]
    in_specs = [pl.BlockSpec((tm, k), lambda s: (mm(s)[0], 0)), w_spec]
    if chunk:
        ins.append(norm_w.reshape(1, chunk).astype(F32))
        in_specs.append(pl.BlockSpec((1, chunk), lambda s: (0, 0)))
    if rope is not None:
        for t in rope:
            ins.append(t)
            in_specs.append(pl.BlockSpec((tm, chunk), lambda s: (ep(s)[0] % rope_rep, 0)))
    out_shape = [jax.ShapeDtypeStruct((m, ncols), dt) for dt in out_dtypes]
    out_specs = [pl.BlockSpec((tm, tn), lambda s: ep(s)) for _ in out_dtypes]
    if emit:
        out_shape.append(jax.ShapeDtypeStruct((k, ncols), BF16))
        out_specs.append(_emit_spec(k, tn, ni, nj))
    n_steps = ni * nj + lead + int(delayed)
    side_in, side_out, side_shape = _side_cast_specs(side_casts, n_steps)
    ins += list(side_casts)
    in_specs += side_in
    out_specs += side_out
    out_shape += side_shape
    body = functools.partial(_proj_body, ni=ni, nj=nj, tn=tn, chunk=chunk, rotary=rope is not None,
                             sigmoid=sigmoid, n_out=len(out_dtypes), delayed=delayed, cast=bool(lead), emit=emit,
                             n_side=len(side_casts))
    if delayed:
        scratch = scratch + [pltpu.VMEM((tm, tn), F32), pltpu.VMEM((tm, tn), F32)]
    res = pl.pallas_call(
        body,
        out_shape=out_shape,
        grid=(n_steps,),
        in_specs=in_specs,
        out_specs=out_specs,
        scratch_shapes=scratch,
        compiler_params=_params(("arbitrary",), STREAM_VMEM_MIB),
        name=name,
    )(*ins)
    return res


def _lam_value(lq1, lk1, lq2, lk2, lam_init):
    a = jnp.sum(lq1[...] * lk1[...], axis=-1, keepdims=True)
    b = jnp.sum(lq2[...] * lk2[...], axis=-1, keepdims=True)
    return jnp.exp(a) - jnp.exp(b) + lam_init


def _subln(o, sw, lam_init):
    ms = jnp.mean(o * o, axis=-1, keepdims=True)
    return (o * lax.rsqrt(ms + EPS) * sw) * (1.0 - lam_init)


def _attn_body(lq1, lk1, lq2, lk2, sw_ref, q_ref, k_ref, v_ref, o_ref,
               m1, l1, a1, m2, l2, a2, *, tq, lam_init, scale):
    qi = pl.program_id(2)
    q = q_ref[...]
    qs = (q[:, :HEAD_DIM], q[:, HEAD_DIM:])
    stats = ((m1, l1, a1), (m2, l2, a2))
    for m, l, a in stats:
        m[...] = jnp.full(m.shape, NEG, F32)
        l[...] = jnp.zeros(l.shape, F32)
        a[...] = jnp.zeros(a.shape, F32)

    def scores(j):
        start = pl.multiple_of(j * tq, tq)
        kb = k_ref[pl.ds(start, tq), :]
        return tuple(_dot_nt(qs[c], kb[:, c * HEAD_DIM:(c + 1) * HEAD_DIM]) for c in range(2))

    def accumulate(j, ss, mask):
        start = pl.multiple_of(j * tq, tq)
        vb = v_ref[pl.ds(start, tq), :]
        for s, (m, l, a) in zip(ss, stats):
            if mask is not None:
                s = jnp.where(mask, s, NEG)
            m_old = m[...]
            m_new = jnp.maximum(m_old, jnp.max(s, axis=-1, keepdims=True))
            alpha = jnp.exp2((m_old - m_new) * (scale * LOG2_E))
            p = jnp.exp2((s - m_new) * (scale * LOG2_E))
            l[...] = alpha * l[...] + jnp.sum(p, axis=-1, keepdims=True)
            a[...] = alpha * a[...] + _dot(p.astype(BF16), vb)
            m[...] = m_new

    def body(j, ss):
        nxt = scores(j + 1)
        accumulate(j, ss, None)
        return nxt

    ss = lax.fori_loop(0, qi, body, scores(0))
    row = lax.broadcasted_iota(jnp.int32, (tq, tq), 0)
    col = lax.broadcasted_iota(jnp.int32, (tq, tq), 1)
    accumulate(qi, ss, col <= row)

    lam = _lam_value(lq1, lk1, lq2, lk2, lam_init)
    o = a1[...] / l1[...] - lam * (a2[...] / l2[...])
    o_ref[...] = _subln(o, sw_ref[...], lam_init).astype(o_ref.dtype)


def _attn_prompt(q, k, v, lam_vecs, subln_w, b, s, lam_init):
    tq = _row_tile(s, 512)
    nq = s // tq
    vec = pl.BlockSpec((1, HEAD_DIM), lambda bi, h, i: (0, 0))
    body = functools.partial(_attn_body, tq=tq, lam_init=lam_init, scale=HEAD_DIM ** -0.5)
    return pl.pallas_call(
        body,
        out_shape=jax.ShapeDtypeStruct((b * s, ATTN_W), BF16),
        grid=(b, N_HEADS, nq),
        in_specs=[vec, vec, vec, vec,
                  pl.BlockSpec((1, V_DIM), lambda bi, h, i: (0, 0)),
                  pl.BlockSpec((tq, V_DIM), lambda bi, h, i: (bi * nq + i, h)),
                  pl.BlockSpec((s, V_DIM), lambda bi, h, i: (bi, h)),
                  pl.BlockSpec((s, V_DIM), lambda bi, h, i: (bi, h))],
        out_specs=pl.BlockSpec((tq, V_DIM), lambda bi, h, i: (bi * nq + i, h)),
        scratch_shapes=[pltpu.VMEM((tq, 1), F32), pltpu.VMEM((tq, 1), F32), pltpu.VMEM((tq, V_DIM), F32),
                        pltpu.VMEM((tq, 1), F32), pltpu.VMEM((tq, 1), F32), pltpu.VMEM((tq, V_DIM), F32)],
        compiler_params=_params(("parallel", "parallel", "arbitrary"), 40),
        name="attn_prompt",
    )(*lam_vecs, subln_w.reshape(1, V_DIM), q, k, v)


def _sattn_body(pt_ref, lq1, lk1, lq2, lk2, sw_ref, q_ref, kn_ref, vn_ref, *rest, npg, t_pad, lam_init, scale):
    del pt_ref
    k_refs = rest[:npg]
    v_refs = rest[npg:2 * npg]
    o_ref = rest[2 * npg]
    m_ref, l_ref, acc_ref = rest[2 * npg + 1:]
    c = pl.program_id(1)
    q = q_ref[...].astype(BF16)
    nrows = q.shape[0]

    def masks(ncols):
        row = lax.broadcasted_iota(jnp.int32, (nrows, ncols), 0)
        col = lax.broadcasted_iota(jnp.int32, (nrows, ncols), 1)
        same_head = (row // (2 * t_pad)) == (col % N_HEADS)
        return same_head, row % t_pad, col // N_HEADS

    def update(kbs, vbs, mask):
        ss = [jnp.where(mask, _dot_nt(q, kb) * scale, NEG) for kb in kbs]
        m_cur = functools.reduce(jnp.maximum, [jnp.max(s, axis=-1, keepdims=True) for s in ss])
        m_old = m_ref[...]
        m_new = jnp.maximum(m_old, m_cur)
        alpha = jnp.exp(m_old - m_new)
        ps = [jnp.exp(s - m_new) for s in ss]
        l_ref[...] = alpha * l_ref[...] + sum(jnp.sum(p, axis=-1, keepdims=True) for p in ps)
        acc_ref[...] = alpha * acc_ref[...] + sum(_dot(p.astype(BF16), vb) for p, vb in zip(ps, vbs))
        m_ref[...] = m_new

    @pl.when(c == 0)
    def _():
        m_ref[...] = jnp.full(m_ref.shape, NEG, F32)
        l_ref[...] = jnp.zeros(l_ref.shape, F32)
        acc_ref[...] = jnp.zeros(acc_ref.shape, F32)
        same_head, step, tok = masks(kn_ref.shape[0])
        update([kn_ref[...]], [vn_ref[...]], same_head & (tok <= step))

    update([r[...].astype(BF16) for r in k_refs], [r[...].astype(BF16) for r in v_refs],
           masks(k_refs[0].shape[0])[0])

    @pl.when(c == pl.num_programs(1) - 1)
    def _():
        lam = _lam_value(lq1, lk1, lq2, lk2, lam_init)
        y = acc_ref[...] / l_ref[...]
        o = y - lam * pltpu.roll(y, nrows - t_pad, 0)
        o_ref[...] = _subln(o, sw_ref[...], lam_init)


def _attn_sample(q32, k_bf, v_bf, cache_k, cache_v, layer, page_table, lam_vecs, subln_w, db, t, lam_init):
    t_pad = SUBLANES // 2
    assert t <= t_pad
    depth, n_phys, page, _, _ = cache_k.shape
    n_pages = page_table.shape[1]
    npg = math.gcd(PAGES_PER_STEP, n_pages)
    ck = cache_k.reshape(depth, n_phys, page * N_HEADS, V_DIM)
    cv = cache_v.reshape(depth, n_phys, page * N_HEADS, V_DIM)
    q5 = q32.reshape(db, t, N_HEADS, 2, HEAD_DIM).transpose(0, 2, 3, 1, 4)
    q5 = jnp.pad(q5, ((0, 0), (0, 0), (0, 0), (0, t_pad - t), (0, 0)))
    eye = jnp.eye(2, dtype=F32)
    nrows = N_HEADS * 2 * t_pad
    qbd = (q5[:, :, :, :, None, :] * eye[None, None, :, None, :, None]).reshape(db, nrows, V_DIM)
    new_tok = LANES // N_HEADS
    pad = ((0, 0), (0, new_tok - t), (0, 0), (0, 0))
    kn = jnp.pad(k_bf.reshape(db, t, N_HEADS, V_DIM), pad).reshape(db, LANES, V_DIM)
    vn = jnp.pad(v_bf.reshape(db, t, N_HEADS, V_DIM), pad).reshape(db, LANES, V_DIM)

    def page_spec(i):
        return pl.BlockSpec((None, None, page * N_HEADS, V_DIM),
                            lambda b, c, pt: (layer, pt[b, c * npg + i], 0, 0))

    vec = pl.BlockSpec((1, HEAD_DIM), lambda b, c, pt: (0, 0))
    per_seq = lambda rows: pl.BlockSpec((None, rows, V_DIM), lambda b, c, pt: (b, 0, 0))
    in_specs = [vec, vec, vec, vec, pl.BlockSpec((1, V_DIM), lambda b, c, pt: (0, 0)),
                per_seq(nrows), per_seq(LANES), per_seq(LANES)]
    in_specs += [page_spec(i) for i in range(npg)] * 2
    body = functools.partial(_sattn_body, npg=npg, t_pad=t_pad, lam_init=lam_init, scale=HEAD_DIM ** -0.5)
    out = pl.pallas_call(
        body,
        out_shape=jax.ShapeDtypeStruct((db, nrows, V_DIM), F32),
        grid_spec=pltpu.PrefetchScalarGridSpec(
            num_scalar_prefetch=1,
            grid=(db, n_pages // npg),
            in_specs=in_specs,
            out_specs=per_seq(nrows),
            scratch_shapes=[pltpu.VMEM((nrows, 1), F32), pltpu.VMEM((nrows, 1), F32),
                            pltpu.VMEM((nrows, V_DIM), F32)]),
        compiler_params=_params(("arbitrary", "arbitrary"), 48),
        name="attn_sample",
    )(page_table, *lam_vecs, subln_w.reshape(1, V_DIM), qbd, kn, vn, *([ck] * npg), *([cv] * npg))
    out = out.reshape(db, N_HEADS, 2, t_pad, V_DIM)[:, :, 0, :t]
    return out.transpose(0, 2, 1, 3).reshape(db * t, ATTN_W)


def _xattn_prompt_body(q_ref, k_ref, v_ref, o_ref, *, scale):
    s = _dot_nt(q_ref[...], k_ref[...]) * scale
    m = jnp.max(s, axis=-1, keepdims=True)
    p = jnp.exp(s - m)
    p = p / jnp.sum(p, axis=-1, keepdims=True)
    o_ref[...] = _dot(p.astype(BF16), v_ref[...]).astype(o_ref.dtype)


def _xattn_prompt(xq, mk, mv, b, s, n_mem):
    tq = _row_tile(s, 1024)
    nq = s // tq
    body = functools.partial(_xattn_prompt_body, scale=X_HEAD_DIM ** -0.5)
    return pl.pallas_call(
        body,
        out_shape=jax.ShapeDtypeStruct((b * s, X_W), BF16),
        grid=(b, X_HEADS, nq),
        in_specs=[pl.BlockSpec((tq, X_HEAD_DIM), lambda bi, h, i: (bi * nq + i, h)),
                  pl.BlockSpec((n_mem, X_HEAD_DIM), lambda bi, h, i: (bi, h)),
                  pl.BlockSpec((n_mem, X_HEAD_DIM), lambda bi, h, i: (bi, h))],
        out_specs=pl.BlockSpec((tq, X_HEAD_DIM), lambda bi, h, i: (bi * nq + i, h)),
        compiler_params=_params(("parallel", "parallel", "parallel"), 32),
        name="xattn_prompt",
    )(xq, mk, mv)


def _xattn_sample_body(q_ref, k_ref, v_ref, o_ref, *, t, scale):
    g, n_mem = k_ref.shape[0], k_ref.shape[1]
    rows = q_ref.shape[0]
    ncols = g * n_mem * X_HEADS
    kb = k_ref[...].reshape(ncols, X_HEAD_DIM).astype(BF16)
    vb = v_ref[...].reshape(ncols, X_HEAD_DIM).astype(BF16)
    row = lax.broadcasted_iota(jnp.int32, (rows, ncols), 0)
    col = lax.broadcasted_iota(jnp.int32, (rows, ncols), 1)
    same = ((row // (t * X_HEADS)) == (col // (n_mem * X_HEADS))) & ((row % X_HEADS) == (col % X_HEADS))
    s = jnp.where(same, _dot_nt(q_ref[...], kb) * scale, NEG)
    m = jnp.max(s, axis=-1, keepdims=True)
    p = jnp.exp(s - m)
    p = p / jnp.sum(p, axis=-1, keepdims=True)
    o_ref[...] = _dot(p.astype(BF16), vb).astype(o_ref.dtype)


def _xattn_sample(xq, mem_k, mem_v, layer, db, t):
    n_mem = mem_k.shape[2]
    g = math.gcd(XATTN_GROUP, db)
    rows = g * t * X_HEADS
    q = xq.reshape(db * t * X_HEADS, X_HEAD_DIM)
    mem_spec = pl.BlockSpec((None, g, n_mem, X_HEADS, X_HEAD_DIM), lambda i: (layer, i, 0, 0, 0))
    body = functools.partial(_xattn_sample_body, t=t, scale=X_HEAD_DIM ** -0.5)
    out = pl.pallas_call(
        body,
        out_shape=jax.ShapeDtypeStruct((db * t * X_HEADS, X_HEAD_DIM), BF16),
        grid=(db // g,),
        in_specs=[pl.BlockSpec((rows, X_HEAD_DIM), lambda i: (i, 0)), mem_spec, mem_spec],
        out_specs=pl.BlockSpec((rows, X_HEAD_DIM), lambda i: (i, 0)),
        compiler_params=_params(("parallel",), 40),
        name="xattn_sample",
    )(q, mem_k, mem_v)
    return out.reshape(db * t, X_W)


POOL_HALO = 16


def _pool_prompt_body(x_ref, halo_ref, pw_ref, ps_ref, o_ref, buf, *, tm, tiles_per_seq):
    i = pl.program_id(0)
    first = (i % tiles_per_seq) == 0
    x = x_ref[...]
    buf[0:POOL_HALO, :] = jnp.where(first, 0.0, halo_ref[...])
    buf[POOL_HALO:, :] = x
    pos = (i % tiles_per_seq) * tm + lax.broadcasted_iota(jnp.int32, (tm, 1), 0)
    for g, w in enumerate(POOL_WINDOWS):
        cols = slice(g * POOL_GC, (g + 1) * POOL_GC)
        xs = x[:, cols]
        acc = xs
        for j in range(1, w):
            acc = acc + buf[POOL_HALO - j:POOL_HALO - j + tm, cols]
        cnt = jnp.minimum(w, pos + 1).astype(F32)
        d = acc / cnt - xs
        o_ref[:, cols] = (_dot(d.astype(BF16), pw_ref[g]) * ps_ref[:, cols]).astype(o_ref.dtype)


def _pool_prompt(pz, pool_w, pool_scale, s):
    m = pz.shape[0]
    tm = _row_tile(s, 512)
    hb = tm // POOL_HALO
    body = functools.partial(_pool_prompt_body, tm=tm, tiles_per_seq=s // tm)
    return pl.pallas_call(
        body,
        out_shape=jax.ShapeDtypeStruct((m, POOL_W), BF16),
        grid=(m // tm,),
        in_specs=[pl.BlockSpec((tm, POOL_W), lambda i: (i, 0)),
                  pl.BlockSpec((POOL_HALO, POOL_W), lambda i: (jnp.maximum(i * hb - 1, 0), 0)),
                  pl.BlockSpec(pool_w.shape, lambda i: (0, 0, 0)),
                  pl.BlockSpec((1, POOL_W), lambda i: (0, 0))],
        out_specs=pl.BlockSpec((tm, POOL_W), lambda i: (i, 0)),
        scratch_shapes=[pltpu.VMEM((tm + POOL_HALO, POOL_W), F32)],
        compiler_params=_params(("parallel",), 32),
        name="pool_prompt",
    )(pz, pz, pool_w, pool_scale.reshape(1, POOL_W))


def _pool_sample_body(st_ref, pz_ref, pw_ref, ps_ref, o_ref, *, n_state, t, past_len):
    def row(idx, cols):
        return st_ref[idx][:, cols] if idx < n_state else pz_ref[idx - n_state][:, cols]

    for g, w in enumerate(POOL_WINDOWS):
        cols = slice(g * POOL_GC, (g + 1) * POOL_GC)
        ds = []
        for ti in range(t):
            hi = n_state + ti + 1
            lo = max(hi - w, 0)
            acc = row(hi - 1, cols)
            for idx in range(hi - 2, lo - 1, -1):
                acc = acc + row(idx, cols)
            cnt = float(min(w, past_len + ti + 1))
            ds.append(acc / cnt - row(hi - 1, cols))
        d = jnp.concatenate(ds, axis=0)
        o_ref[:, cols] = (_dot(d.astype(BF16), pw_ref[g]) * ps_ref[:, cols]).astype(o_ref.dtype)


def _pool_sample(pz, state_pool, pool_w, pool_scale, db, t, past_len):
    n_state = state_pool.shape[1]
    st = state_pool.transpose(1, 0, 2)
    pzt = pz.reshape(db, t, POOL_W).transpose(1, 0, 2)
    body = functools.partial(_pool_sample_body, n_state=n_state, t=t, past_len=past_len)
    out = pl.pallas_call(
        body,
        out_shape=jax.ShapeDtypeStruct((t * db, POOL_W), BF16),
        grid=(1,),
        in_specs=[pl.BlockSpec(st.shape, lambda i: (0, 0, 0)),
                  pl.BlockSpec(pzt.shape, lambda i: (0, 0, 0)),
                  pl.BlockSpec(pool_w.shape, lambda i: (0, 0, 0)),
                  pl.BlockSpec((1, POOL_W), lambda i: (0, 0))],
        out_specs=pl.BlockSpec((t * db, POOL_W), lambda i: (0, 0)),
        compiler_params=_params(("arbitrary",), 40),
        name="pool_sample",
    )(st, pzt, pool_w, pool_scale.reshape(1, POOL_W))
    return out.reshape(t, db, POOL_W).transpose(1, 0, 2).reshape(db * t, POOL_W)


def _merge_body(a_ref, p_ref, c_ref, wa_ref, wp_ref, wc_ref, g0_ref, g1_ref, g2_ref, o_ref):
    m = (g0_ref[...] * _dot(a_ref[...], wa_ref[...])
         + g1_ref[...] * _dot(p_ref[...], wp_ref[...])
         + g2_ref[...] * _dot(c_ref[...], wc_ref[...]))
    o_ref[...] = m.astype(o_ref.dtype)


def _merge(a, po, c, gates, wa, wp, wc, d, tm):
    m = a.shape[0]
    assert m % tm == 0
    tn = 512
    nj = d // tn
    row = lambda k: pl.BlockSpec((tm, k), lambda i, j: (i, 0))
    wcol = lambda k: pl.BlockSpec((k, tn), lambda i, j: (0, j))
    gate = lambda r: pl.BlockSpec((tm, tn), lambda i, j: (i, r * nj + j))
    return pl.pallas_call(
        _merge_body,
        out_shape=jax.ShapeDtypeStruct((m, d), BF16),
        grid=(m // tm, nj),
        in_specs=[row(ATTN_W), row(POOL_W), row(X_W), wcol(ATTN_W), wcol(POOL_W), wcol(X_W),
                  gate(0), gate(1), gate(2)],
        out_specs=pl.BlockSpec((tm, tn), lambda i, j: (i, j)),
        compiler_params=_params(("parallel", "arbitrary"), 48),
        name="merge",
    )(a, po, c, wa, wp, wc, gates, gates, gates)


def _resid_body(x_ref, w_ref, r_ref, o_ref):
    o_ref[...] = r_ref[...] + _dot(x_ref[...], w_ref[...])


def _resid_proj(x, w, resid, name, *, tm, tn):
    m, k = x.shape
    n = w.shape[1]
    assert m % tm == 0 and n % tn == 0
    return pl.pallas_call(
        _resid_body,
        out_shape=jax.ShapeDtypeStruct((m, n), F32),
        grid=(m // tm, n // tn),
        in_specs=[pl.BlockSpec((tm, k), lambda i, j: (i, 0)),
                  pl.BlockSpec((k, tn), lambda i, j: (0, j)),
                  pl.BlockSpec((tm, tn), lambda i, j: (i, j))],
        out_specs=pl.BlockSpec((tm, tn), lambda i, j: (i, j)),
        compiler_params=_params(("parallel", "arbitrary"), STREAM_VMEM_MIB),
        name=name,
    )(x, w, resid)


CONV_HALO = SUBLANES


def _ffn_a_body(*refs, ni, nj, tm, tiles_per_seq, seq_len, has_prev, cast, emit, n_side):
    h_ref, wg_ref, wu_ref, cw_ref, cb_ref = refs[:5]
    idx = 5
    if has_prev:
        p1_ref, p2_ref = refs[idx:idx + 2]
        idx += 2
    side_in = refs[idx:idx + n_side]
    idx += n_side
    act_ref, g_ref = refs[idx:idx + 2]
    idx += 2
    side_out = refs[idx + 2 * int(emit):idx + 2 * int(emit) + n_side]
    scratch = refs[idx + 2 * int(emit) + n_side:]
    s = pl.program_id(0)
    for src, dst in zip(side_in, side_out):
        dst[...] = src[...].astype(BF16)
    if cast:
        wgb, wub, buf, carry = scratch
        cur = _cast_weight_chunk(s, ni, wg_ref, wgb)
        _cast_weight_chunk(s, ni, wu_ref, wub)
        rhs = (lambda: wgb[cur]), (lambda: wub[cur])
        t = s - ni
        if emit:
            @pl.when((s % ni == 0) & (s >= ni) & (s <= nj * ni))
            def _():
                refs[idx][...] = wgb[cur]
                refs[idx + 1][...] = wub[cur]
    else:
        buf, carry = scratch
        rhs = (lambda: wg_ref[...]), (lambda: wu_ref[...])
        t = s

    @pl.when(s == 0)
    def _():
        carry[...] = jnp.zeros(carry.shape, F32)

    @pl.when(t >= 0)
    def _():
        _ffn_a_tile(s, rhs, h_ref, cw_ref, cb_ref, p1_ref if has_prev else None,
                    p2_ref if has_prev else None, act_ref, g_ref, buf, carry, ni=ni, tm=tm,
                    tiles_per_seq=tiles_per_seq, seq_len=seq_len)


def _ffn_a_tile(s, rhs, h_ref, cw_ref, cb_ref, p1_ref, p2_ref, act_ref, g_ref, buf, carry, *,
                ni, tm, tiles_per_seq, seq_len):
    has_prev = p1_ref is not None
    h = h_ref[...]
    g = _dot(h, rhs[0]())
    u = _dot(h, rhs[1]())
    if tiles_per_seq:
        first = ((s % ni) % tiles_per_seq) == 0
        buf[0:CONV_HALO, :] = jnp.where(first, 0.0, carry[...])
    else:
        buf[0:CONV_HALO, :] = jnp.zeros((CONV_HALO, g.shape[1]), F32)
    buf[CONV_HALO:, :] = g
    carry[...] = g[tm - CONV_HALO:, :]
    g1 = buf[CONV_HALO - 1:CONV_HALO - 1 + tm, :]
    g2 = buf[CONV_HALO - 2:CONV_HALO - 2 + tm, :]
    if not tiles_per_seq:
        t = lax.broadcasted_iota(jnp.int32, (tm, 1), 0) % seq_len
        g1 = jnp.where(t >= 1, g1, 0.0)
        g2 = jnp.where(t >= 2, g2, 0.0)
    if has_prev:
        g1 = g1 + p1_ref[...]
        g2 = g2 + p2_ref[...]
    cw = cw_ref[...]
    gc = cb_ref[...] + g2 * cw[0:1, :]
    gc = gc + g1 * cw[1:2, :]
    gc = gc + g * cw[2:3, :]
    act_ref[...] = (jax.nn.silu(gc) * u).astype(act_ref.dtype)
    if tiles_per_seq:
        g_ref[...] = g[tm - CONV_HALO:, :]
    else:
        g_ref[...] = g


def _ffn_a(h2, wg, wu, conv_w, conv_b, seq_len, tm, prev=None, emit=False, side_casts=()):
    m, d = h2.shape
    dff = wg.shape[1]
    tn = 256
    assert dff % tn == 0 and m % tm == 0
    long_seq = seq_len >= tm
    if long_seq:
        assert seq_len % tm == 0 and prev is None
        tiles_per_seq = seq_len // tm
    else:
        assert tm == m and tm % seq_len == 0
        tiles_per_seq = 0
    ni, nj = m // tm, dff // tn
    w_spec, w_scratch, lead = _weight_specs(wg, d, tn, ni, nj, 0)
    assert wu.dtype == wg.dtype and (not emit or lead)
    tile = _tile_maps(ni, nj, lead, 0)
    col = lambda s: (0, tile(s)[1])
    ins = [h2, wg, wu, conv_w, conv_b.reshape(1, dff)]
    in_specs = [pl.BlockSpec((tm, d), lambda s: (tile(s)[0], 0)),
                w_spec,
                w_spec,
                pl.BlockSpec((CONV_W, tn), col),
                pl.BlockSpec((1, tn), col)]
    if prev is not None:
        ins += list(prev)
        in_specs += [pl.BlockSpec((tm, tn), tile)] * 2
    if long_seq:
        g_shape = jax.ShapeDtypeStruct((ni, CONV_HALO, dff), F32)
        g_spec = pl.BlockSpec((None, CONV_HALO, tn), lambda s: (tile(s)[0], 0, tile(s)[1]))
    else:
        g_shape = jax.ShapeDtypeStruct((m, dff), F32)
        g_spec = pl.BlockSpec((tm, tn), tile)
    body = functools.partial(_ffn_a_body, ni=ni, nj=nj, tm=tm, tiles_per_seq=tiles_per_seq, seq_len=seq_len,
                             has_prev=prev is not None, cast=bool(lead), emit=emit, n_side=len(side_casts))
    out_shape = [jax.ShapeDtypeStruct((m, dff), BF16), g_shape]
    out_specs = [pl.BlockSpec((tm, tn), tile), g_spec]
    if emit:
        out_shape += [jax.ShapeDtypeStruct((d, dff), BF16)] * 2
        out_specs += [_emit_spec(d, tn, ni, nj)] * 2
    n_steps = ni * nj + lead
    side_in, side_out, side_shape = _side_cast_specs(side_casts, n_steps)
    ins += list(side_casts)
    in_specs += side_in
    out_specs += side_out
    out_shape += side_shape
    return pl.pallas_call(
        body,
        out_shape=out_shape,
        grid=(n_steps,),
        in_specs=in_specs,
        out_specs=out_specs,
        scratch_shapes=w_scratch + w_scratch + [pltpu.VMEM((tm + CONV_HALO, tn), F32),
                                                pltpu.VMEM((CONV_HALO, tn), F32)],
        compiler_params=_params(("arbitrary",), STREAM_VMEM_MIB),
        name="ffn_gate_up",
    )(*ins)


def _rope_tables(pos):
    half = HEAD_DIM // 2
    inv = ROPE_THETA ** (-jnp.arange(half, dtype=F32) / half)
    ang = pos.astype(F32)[:, None] * inv[None, :]
    cos, sin = jnp.cos(ang), jnp.sin(ang)
    return jnp.concatenate([cos, cos], axis=-1), jnp.concatenate([-sin, sin], axis=-1)


def _layer(x2, w, lam_init, *, tm, seq_len, rope, attn_fn, pool_fn, xattn_fn, conv_prev, q_dtype, wbf=None):
    d = x2.shape[1]
    rope_rep = rope[0].shape[0] // tm
    pool_off = 3 * ATTN_W
    xq_off = pool_off + POOL_W
    gate_off = xq_off + X_W
    emit = wbf is None
    new_wbf = {}

    def in_proj(key, col0, ncols, out_dtypes, side=(), **kw):
        if emit:
            res = _proj(h, w['w_in'], col0, ncols, out_dtypes, "proj_" + key, tm=tm, emit=True,
                        side_casts=[w[name] for name in side], **kw)
            new_wbf[key] = res[len(out_dtypes)]
            new_wbf.update(zip(side, res[len(out_dtypes) + 1:]))
            return res[:len(out_dtypes)]
        return _proj(h, wbf[key], 0, ncols, out_dtypes, "proj_" + key, tm=tm, **kw)

    h = _rmsnorm(x2, w['attn_norm_w'], "attn_norm")
    (q,) = in_proj('q', 0, ATTN_W, [q_dtype], norm_w=w['q_norm_w'], rope=rope, rope_rep=rope_rep)
    k32, kbf = in_proj('k', ATTN_W, ATTN_W, [F32, BF16], norm_w=w['k_norm_w'], rope=rope, rope_rep=rope_rep)
    v32, vbf = in_proj('v', 2 * ATTN_W, ATTN_W, [F32, BF16])
    (pz,) = in_proj('pool', pool_off, POOL_W, [F32])
    (xq,) = in_proj('xq', xq_off, X_W, [BF16], norm_w=w['xq_norm_w'])
    (gates,) = in_proj('gates', gate_off, 3 * d, [F32], sigmoid=True,
                       side=('w_out', 'w_branch_attn', 'w_branch_pool', 'w_branch_cross'))
    a = attn_fn(q, kbf, vbf)
    po = pool_fn(pz)
    c = xattn_fn(xq)
    if emit:
        act_w = new_wbf
    else:
        act_w = wbf
    m = _merge(a, po, c, gates, act_w['w_branch_attn'], act_w['w_branch_pool'], act_w['w_branch_cross'], d, tm)
    x1 = _resid_proj(m, act_w['w_out'], x2, "out_proj", tm=tm, tn=512)
    h2 = _rmsnorm(x1, w['ffn_norm_w'], "ffn_norm")
    if emit:
        act, gl, new_wbf['gate'], new_wbf['up'], new_wbf['w_down'] = _ffn_a(
            h2, w['w_gate'], w['w_up'], w['conv_w'], w['conv_b'], seq_len, tm, conv_prev, emit=True,
            side_casts=[w['w_down']])
    else:
        act, gl = _ffn_a(h2, wbf['gate'], wbf['up'], w['conv_w'], w['conv_b'], seq_len, tm, conv_prev)
    y = _resid_proj(act, act_w['w_down'], x1, "down_proj", tm=min(tm, 512), tn=512)
    return y, k32, v32, pz, gl, new_wbf


def kernel(x_prompt, x_sample, cache_k, cache_v, cache_mem_k, cache_mem_v, state_pool, state_conv, page_table, mem_prompt, attn_norm_w, w_in, q_norm_w, k_norm_w, lambda_q1, lambda_k1, lambda_q2, lambda_k2, subln_w, pool_w, pool_scale, mem_norm_w, w_mem_k, w_mem_v, xq_norm_w, xk_norm_w, w_branch_attn, w_branch_pool, w_branch_cross, w_out, ffn_norm_w, w_gate, w_up, conv_w, conv_b, w_down):
    b, s, d = x_prompt.shape
    db, t, _ = x_sample.shape
    depth = w_in.shape[0]
    dff = w_gate.shape[-1]
    n_mem = mem_prompt.shape[1]
    past_len = page_table.shape[1] * cache_k.shape[2]
    n_state = state_pool.shape[2]
    rope_p = _rope_tables(jnp.arange(s, dtype=jnp.int32))
    rope_s = _rope_tables(jnp.tile(past_len + jnp.arange(t, dtype=jnp.int32), db))
    yp = x_prompt.reshape(b * s, d)
    ys = x_sample.reshape(db * t, d)
    outs = [[] for _ in range(10)]
    for l in range(depth):
        lam_init = 0.8 - 0.6 * math.exp(-0.3 * l)
        w = {'attn_norm_w': attn_norm_w[l], 'w_in': w_in[l], 'q_norm_w': q_norm_w[l],
             'k_norm_w': k_norm_w[l], 'xq_norm_w': xq_norm_w[l],
             'w_branch_attn': w_branch_attn[l], 'w_branch_pool': w_branch_pool[l],
             'w_branch_cross': w_branch_cross[l], 'w_out': w_out[l],
             'ffn_norm_w': ffn_norm_w[l], 'w_gate': w_gate[l], 'w_up': w_up[l],
             'conv_w': conv_w[l], 'conv_b': conv_b[l], 'w_down': w_down[l]}
        lam_vecs = [v[l].reshape(1, HEAD_DIM).astype(F32) for v in (lambda_q1, lambda_k1, lambda_q2, lambda_k2)]
        pw = pool_w[l].astype(BF16)
        ps = pool_scale[l]
        sw = subln_w[l]

        mn = _rmsnorm(mem_prompt.reshape(b * n_mem, d), mem_norm_w[l], "mem_norm")
        tm_m = _row_tile(b * n_mem, 1024)
        mk32, mkbf = _proj(mn, w_mem_k[l], 0, X_W, [F32, BF16], "proj_mem_k", tm=tm_m,
                           norm_w=xk_norm_w[l])
        mv32, mvbf = _proj(mn, w_mem_v[l], 0, X_W, [F32, BF16], "proj_mem_v", tm=tm_m)
        tm_p = _row_tile(s, 1024)
        yp, k_p, v_p, pz_p, gl_p, wbf = _layer(
            yp, w, lam_init, tm=tm_p, seq_len=s, rope=rope_p,
            attn_fn=lambda q, k, v: _attn_prompt(q, k, v, lam_vecs, sw, b, s, lam_init),
            pool_fn=lambda pz: _pool_prompt(pz, pw, ps, s),
            xattn_fn=lambda xq: _xattn_prompt(xq, mkbf, mvbf, b, s, n_mem),
            conv_prev=None, q_dtype=BF16)
        tiles_per_seq = s // tm_p
        conv_p = gl_p.reshape(b, tiles_per_seq, CONV_HALO, dff)[:, -1, CONV_HALO - (CONV_W - 1):]

        sc = state_conv[l]
        zrow = jnp.zeros((db, 1, dff), F32)
        p1 = jnp.concatenate([sc[:, 1:2]] + [zrow] * (t - 1), axis=1).reshape(db * t, dff)
        p2 = jnp.concatenate([sc[:, 0:1], sc[:, 1:2]] + [zrow] * (t - 2), axis=1).reshape(db * t, dff)
        sp = state_pool[l]
        ys, k_s, v_s, pz_s, g_s, _ = _layer(
            ys, w, lam_init, tm=_row_tile(db * t, 1024), seq_len=t, rope=rope_s,
            attn_fn=lambda q, k, v: _attn_sample(q, k, v, cache_k, cache_v, l, page_table, lam_vecs, sw, db, t,
                                                 lam_init).astype(BF16),
            pool_fn=lambda pz: _pool_sample(pz, sp, pw, ps, db, t, past_len),
            xattn_fn=lambda xq: _xattn_sample(xq, cache_mem_k, cache_mem_v, l, db, t),
            conv_prev=(p1, p2), q_dtype=F32, wbf=wbf)
        pool_s = jnp.concatenate([sp, pz_s.reshape(db, t, POOL_W)], axis=1)[:, -n_state:]
        conv_s = jnp.concatenate([sc, g_s.reshape(db, t, dff)], axis=1)[:, -(CONV_W - 1):]

        vals = [k_p.reshape(b, s, N_HEADS, V_DIM), v_p.reshape(b, s, N_HEADS, V_DIM),
                mk32.reshape(b, n_mem, X_HEADS, X_HEAD_DIM), mv32.reshape(b, n_mem, X_HEADS, X_HEAD_DIM),
                pz_p.reshape(b, s, POOL_W)[:, -n_state:], conv_p,
                k_s.reshape(db, t, N_HEADS, V_DIM), v_s.reshape(db, t, N_HEADS, V_DIM), pool_s, conv_s]
        for o, v in zip(outs, vals):
            o.append(v)
    return (yp.reshape(b, s, d), ys.reshape(db, t, d)) + tuple(jnp.stack(o) for o in outs)
```

```python
import functools
import math

import jax
import jax.numpy as jnp
from jax import lax
from jax.experimental import pallas as pl
from jax.experimental.pallas import tpu as pltpu

F32 = jnp.float32
BF16 = jnp.bfloat16

N_HEADS = 8
HEAD_DIM = 128
V_DIM = 2 * HEAD_DIM
ATTN_W = N_HEADS * V_DIM
POOL_WINDOWS = (2, 4, 8, 16)
POOL_GC = 256
POOL_W = len(POOL_WINDOWS) * POOL_GC
X_HEADS = 4
X_HEAD_DIM = 256
X_W = X_HEADS * X_HEAD_DIM
CONV_W = 3
ROPE_THETA = 10000.0
EPS = 1e-6
NEG = -1e30
LOG2_E = math.log2(math.e)

SUBLANES = 8
BF16_ROWS = 16
LANES = 128
PAGES_PER_STEP = 8
XATTN_GROUP = 4
MIB = 2 ** 20
STREAM_VMEM_MIB = 56


def _params(sem, vmem_mib):
    return pltpu.CompilerParams(dimension_semantics=sem, vmem_limit_bytes=vmem_mib * MIB)


def _row_tile(m, cap):
    t = min(m, cap)
    assert m % t == 0, (m, t)
    return t


def _dot(a, b):
    return jnp.dot(a, b, preferred_element_type=F32)


def _dot_nt(a, b):
    return lax.dot_general(a, b, (((1,), (1,)), ((), ())), preferred_element_type=F32)


def _rmsnorm_body(x_ref, w_ref, o_ref):
    x = x_ref[...]
    ms = jnp.mean(x * x, axis=-1, keepdims=True)
    o_ref[...] = (x * lax.rsqrt(ms + EPS) * w_ref[...]).astype(o_ref.dtype)


def _rmsnorm(x, w, name):
    m, d = x.shape
    tm = _row_tile(m, 256)
    return pl.pallas_call(
        _rmsnorm_body,
        out_shape=jax.ShapeDtypeStruct((m, d), BF16),
        grid=(m // tm,),
        in_specs=[pl.BlockSpec((tm, d), lambda i: (i, 0)), pl.BlockSpec((1, d), lambda i: (0, 0))],
        out_specs=pl.BlockSpec((tm, d), lambda i: (i, 0)),
        compiler_params=_params(("parallel",), 32),
        name=name,
    )(x, w.reshape(1, d))


def _tile_maps(ni, nj, lead, lag):
    def index(s):
        t = jnp.clip(s - lead - lag, 0, ni * nj - 1)
        return t % ni, t // ni

    return index


def _cast_weight_chunk(s, ni, w_ref, wbf):
    kc = w_ref.shape[0]
    slot = (s // ni) % 2
    r0 = pl.multiple_of((s % ni) * kc, kc)
    wbf[slot, pl.ds(r0, kc), :] = w_ref[...].astype(BF16)
    return 1 - slot


def _weight_specs(w, k, tn, ni, nj, cb0):
    if w.dtype == BF16:
        return pl.BlockSpec((k, tn), lambda s: (0, cb0 + jnp.minimum(s // ni, nj - 1))), [], 0
    assert k % (ni * BF16_ROWS) == 0
    spec = pl.BlockSpec((k // ni, tn), lambda s: (s % ni, cb0 + jnp.minimum(s // ni, nj - 1)))
    return spec, [pltpu.VMEM((2, k, tn), BF16)], ni


def _emit_spec(k, tn, ni, nj):
    return pl.BlockSpec((k, tn), lambda s: (0, jnp.clip(s // ni - 1, 0, nj - 1)))


def _side_cast_specs(arrays, n_steps):
    in_specs, out_specs, out_shape = [], [], []
    for a in arrays:
        rows, cols = a.shape
        rb = BF16_ROWS
        while rows % rb or rows // rb > n_steps:
            rb += BF16_ROWS
        index = lambda s, last=rows // rb - 1: (jnp.minimum(s, last), 0)
        in_specs.append(pl.BlockSpec((rb, cols), index))
        out_specs.append(pl.BlockSpec((rb, cols), index))
        out_shape.append(jax.ShapeDtypeStruct((rows, cols), BF16))
    return in_specs, out_specs, out_shape


def _proj_body(*refs, ni, nj, tn, chunk, rotary, sigmoid, n_out, delayed, cast, emit, n_side):
    x_ref, w_ref = refs[0], refs[1]
    idx = 2
    nw_ref = cos_ref = sin_ref = None
    if chunk:
        nw_ref = refs[idx]
        idx += 1
    if rotary:
        cos_ref, sin_ref = refs[idx], refs[idx + 1]
        idx += 2
    side_in = refs[idx:idx + n_side]
    idx += n_side
    outs = refs[idx:idx + n_out]
    idx += n_out
    side_out = refs[idx + int(emit):idx + int(emit) + n_side]
    scratch = refs[idx + int(emit) + n_side:]
    s = pl.program_id(0)
    for src, dst in zip(side_in, side_out):
        dst[...] = src[...].astype(BF16)
    if cast:
        wbf = scratch[0]
        scratch = scratch[1:]
        cur = _cast_weight_chunk(s, ni, w_ref, wbf)
        rhs = lambda: wbf[cur]
        t = s - ni
        if emit:
            @pl.when((s % ni == 0) & (s >= ni) & (s <= nj * ni))
            def _():
                refs[idx][...] = wbf[cur]
    else:
        rhs = lambda: w_ref[...]
        t = s

    def epilogue(z):
        if sigmoid:
            z = jax.nn.sigmoid(z)
        if not chunk:
            for o in outs:
                o[...] = z.astype(o.dtype)
            return
        nw = nw_ref[...]
        for c in range(tn // chunk):
            cols = slice(c * chunk, (c + 1) * chunk)
            zc = z[:, cols]
            ms = jnp.mean(zc * zc, axis=-1, keepdims=True)
            y = zc * lax.rsqrt(ms + EPS) * nw
            if rotary:
                y = y * cos_ref[...] + pltpu.roll(y, chunk // 2, 1) * sin_ref[...]
            for o in outs:
                o[:, cols] = y.astype(o.dtype)

    if not delayed:
        @pl.when(t >= 0)
        def _():
            epilogue(_dot(x_ref[...], rhs()))
        return

    za, zb = scratch

    @pl.when(t == 0)
    def _():
        zb[...] = jnp.zeros(zb.shape, F32)

    def step(z_w, z_r):
        z_w[...] = _dot(x_ref[...], rhs())
        epilogue(z_r[...])

    @pl.when((t >= 0) & (t % 2 == 0))
    def _():
        step(za, zb)

    @pl.when((t >= 0) & (t % 2 == 1))
    def _():
        step(zb, za)


def _proj(x, w, col0, ncols, out_dtypes, name, *, tm, norm_w=None, rope=None, rope_rep=1, sigmoid=False,
          emit=False, side_casts=(), tn=512):
    m, k = x.shape
    assert m % tm == 0 and ncols % tn == 0 and col0 % tn == 0
    cb0 = col0 // tn
    ni, nj = m // tm, ncols // tn
    chunk = 0 if norm_w is None else norm_w.shape[-1]
    delayed = bool(chunk)
    w_spec, scratch, lead = _weight_specs(w, k, tn, ni, nj, cb0)
    assert not emit or lead
    mm = _tile_maps(ni, nj, lead, 0)
    ep = _tile_maps(ni, nj, lead, int(delayed))
    ins = [x, w]
    in_specs = [pl.BlockSpec((tm, k), lambda s: (mm(s)[0], 0)), w_spec]
    if chunk:
        ins.append(norm_w.reshape(1, chunk).astype(F32))
        in_specs.append(pl.BlockSpec((1, chunk), lambda s: (0, 0)))
    if rope is not None:
        for t in rope:
            ins.append(t)
            in_specs.append(pl.BlockSpec((tm, chunk), lambda s: (ep(s)[0] % rope_rep, 0)))
    out_shape = [jax.ShapeDtypeStruct((m, ncols), dt) for dt in out_dtypes]
    out_specs = [pl.BlockSpec((tm, tn), lambda s: ep(s)) for _ in out_dtypes]
    if emit:
        out_shape.append(jax.ShapeDtypeStruct((k, ncols), BF16))
        out_specs.append(_emit_spec(k, tn, ni, nj))
    n_steps = ni * nj + lead + int(delayed)
    side_in, side_out, side_shape = _side_cast_specs(side_casts, n_steps)
    ins += list(side_casts)
    in_specs += side_in
    out_specs += side_out
    out_shape += side_shape
    body = functools.partial(_proj_body, ni=ni, nj=nj, tn=tn, chunk=chunk, rotary=rope is not None,
                             sigmoid=sigmoid, n_out=len(out_dtypes), delayed=delayed, cast=bool(lead), emit=emit,
                             n_side=len(side_casts))
    if delayed:
        scratch = scratch + [pltpu.VMEM((tm, tn), F32), pltpu.VMEM((tm, tn), F32)]
    res = pl.pallas_call(
        body,
        out_shape=out_shape,
        grid=(n_steps,),
        in_specs=in_specs,
        out_specs=out_specs,
        scratch_shapes=scratch,
        compiler_params=_params(("arbitrary",), STREAM_VMEM_MIB),
        name=name,
    )(*ins)
    return res


def _lam_value(lq1, lk1, lq2, lk2, lam_init):
    a = jnp.sum(lq1[...] * lk1[...], axis=-1, keepdims=True)
    b = jnp.sum(lq2[...] * lk2[...], axis=-1, keepdims=True)
    return jnp.exp(a) - jnp.exp(b) + lam_init


def _subln(o, sw, lam_init):
    ms = jnp.mean(o * o, axis=-1, keepdims=True)
    return (o * lax.rsqrt(ms + EPS) * sw) * (1.0 - lam_init)


def _attn_body(lq1, lk1, lq2, lk2, sw_ref, q_ref, k_ref, v_ref, o_ref,
               m1, l1, a1, m2, l2, a2, *, tq, lam_init, scale):
    qi = pl.program_id(2)
    q = q_ref[...]
    qs = (q[:, :HEAD_DIM], q[:, HEAD_DIM:])
    stats = ((m1, l1, a1), (m2, l2, a2))
    for m, l, a in stats:
        m[...] = jnp.full(m.shape, NEG, F32)
        l[...] = jnp.zeros(l.shape, F32)
        a[...] = jnp.zeros(a.shape, F32)

    def scores(j):
        start = pl.multiple_of(j * tq, tq)
        kb = k_ref[pl.ds(start, tq), :]
        return tuple(_dot_nt(qs[c], kb[:, c * HEAD_DIM:(c + 1) * HEAD_DIM]) for c in range(2))

    def accumulate(j, ss, mask):
        start = pl.multiple_of(j * tq, tq)
        vb = v_ref[pl.ds(start, tq), :]
        for s, (m, l, a) in zip(ss, stats):
            if mask is not None:
                s = jnp.where(mask, s, NEG)
            m_old = m[...]
            m_new = jnp.maximum(m_old, jnp.max(s, axis=-1, keepdims=True))
            alpha = jnp.exp2((m_old - m_new) * (scale * LOG2_E))
            p = jnp.exp2((s - pltpu.repeat(m_new, tq // LANES, axis=1)) * (scale * LOG2_E))
            l[...] = alpha * l[...] + jnp.sum(p, axis=-1, keepdims=True)
            a[...] = pltpu.repeat(alpha, V_DIM // LANES, axis=1) * a[...] + _dot(p.astype(BF16), vb)
            m[...] = m_new

    def body(j, ss):
        nxt = scores(j + 1)
        accumulate(j, ss, None)
        return nxt

    ss = lax.fori_loop(0, qi, body, scores(0))
    row = lax.broadcasted_iota(jnp.int32, (tq, tq), 0)
    col = lax.broadcasted_iota(jnp.int32, (tq, tq), 1)
    accumulate(qi, ss, col <= row)

    lam = _lam_value(lq1, lk1, lq2, lk2, lam_init)
    o = a1[...] / l1[:, 0:1] - lam * (a2[...] / l2[:, 0:1])
    o_ref[...] = _subln(o, sw_ref[...], lam_init).astype(o_ref.dtype)


def _attn_prompt(q, k, v, lam_vecs, subln_w, b, s, lam_init):
    tq = _row_tile(s, 512)
    nq = s // tq
    vec = pl.BlockSpec((1, HEAD_DIM), lambda bi, h, i: (0, 0))
    body = functools.partial(_attn_body, tq=tq, lam_init=lam_init, scale=HEAD_DIM ** -0.5)
    return pl.pallas_call(
        body,
        out_shape=jax.ShapeDtypeStruct((b * s, ATTN_W), BF16),
        grid=(b, N_HEADS, nq),
        in_specs=[vec, vec, vec, vec,
                  pl.BlockSpec((1, V_DIM), lambda bi, h, i: (0, 0)),
                  pl.BlockSpec((tq, V_DIM), lambda bi, h, i: (bi * nq + i, h)),
                  pl.BlockSpec((s, V_DIM), lambda bi, h, i: (bi, h)),
                  pl.BlockSpec((s, V_DIM), lambda bi, h, i: (bi, h))],
        out_specs=pl.BlockSpec((tq, V_DIM), lambda bi, h, i: (bi * nq + i, h)),
        scratch_shapes=[pltpu.VMEM((tq, LANES), F32), pltpu.VMEM((tq, LANES), F32), pltpu.VMEM((tq, V_DIM), F32),
                        pltpu.VMEM((tq, LANES), F32), pltpu.VMEM((tq, LANES), F32), pltpu.VMEM((tq, V_DIM), F32)],
        compiler_params=_params(("parallel", "parallel", "arbitrary"), 40),
        name="attn_prompt",
    )(*lam_vecs, subln_w.reshape(1, V_DIM), q, k, v)


def _sattn_body(pt_ref, lq1, lk1, lq2, lk2, sw_ref, q_ref, kn_ref, vn_ref, *rest, npg, t_pad, lam_init, scale):
    del pt_ref
    k_refs = rest[:npg]
    v_refs = rest[npg:2 * npg]
    o_ref = rest[2 * npg]
    m_ref, l_ref, acc_ref = rest[2 * npg + 1:]
    c = pl.program_id(1)
    q = q_ref[...].astype(BF16)
    nrows = q.shape[0]

    def masks(ncols):
        row = lax.broadcasted_iota(jnp.int32, (nrows, ncols), 0)
        col = lax.broadcasted_iota(jnp.int32, (nrows, ncols), 1)
        same_head = (row // (2 * t_pad)) == (col % N_HEADS)
        return same_head, row % t_pad, col // N_HEADS

    def update(kbs, vbs, mask):
        ss = [jnp.where(mask, _dot_nt(q, kb) * scale, NEG) for kb in kbs]
        m_cur = functools.reduce(jnp.maximum, [jnp.max(s, axis=-1, keepdims=True) for s in ss])
        m_old = m_ref[...]
        m_new = jnp.maximum(m_old, m_cur)
        alpha = jnp.exp(m_old - m_new)
        ps = [jnp.exp(s - m_new) for s in ss]
        l_ref[...] = alpha * l_ref[...] + sum(jnp.sum(p, axis=-1, keepdims=True) for p in ps)
        acc_ref[...] = alpha * acc_ref[...] + sum(_dot(p.astype(BF16), vb) for p, vb in zip(ps, vbs))
        m_ref[...] = m_new

    @pl.when(c == 0)
    def _():
        m_ref[...] = jnp.full(m_ref.shape, NEG, F32)
        l_ref[...] = jnp.zeros(l_ref.shape, F32)
        acc_ref[...] = jnp.zeros(acc_ref.shape, F32)
        same_head, step, tok = masks(kn_ref.shape[0])
        update([kn_ref[...]], [vn_ref[...]], same_head & (tok <= step))

    update([r[...].astype(BF16) for r in k_refs], [r[...].astype(BF16) for r in v_refs],
           masks(k_refs[0].shape[0])[0])

    @pl.when(c == pl.num_programs(1) - 1)
    def _():
        lam = _lam_value(lq1, lk1, lq2, lk2, lam_init)
        y = acc_ref[...] / l_ref[...]
        o = y - lam * pltpu.roll(y, nrows - t_pad, 0)
        o_ref[...] = _subln(o, sw_ref[...], lam_init)


def _attn_sample(q32, k_bf, v_bf, cache_k, cache_v, layer, page_table, lam_vecs, subln_w, db, t, lam_init):
    t_pad = SUBLANES // 2
    assert t <= t_pad
    depth, n_phys, page, _, _ = cache_k.shape
    n_pages = page_table.shape[1]
    npg = math.gcd(PAGES_PER_STEP, n_pages)
    ck = cache_k.reshape(depth, n_phys, page * N_HEADS, V_DIM)
    cv = cache_v.reshape(depth, n_phys, page * N_HEADS, V_DIM)
    q5 = q32.reshape(db, t, N_HEADS, 2, HEAD_DIM).transpose(0, 2, 3, 1, 4)
    q5 = jnp.pad(q5, ((0, 0), (0, 0), (0, 0), (0, t_pad - t), (0, 0)))
    eye = jnp.eye(2, dtype=F32)
    nrows = N_HEADS * 2 * t_pad
    qbd = (q5[:, :, :, :, None, :] * eye[None, None, :, None, :, None]).reshape(db, nrows, V_DIM)
    new_tok = LANES // N_HEADS
    pad = ((0, 0), (0, new_tok - t), (0, 0), (0, 0))
    kn = jnp.pad(k_bf.reshape(db, t, N_HEADS, V_DIM), pad).reshape(db, LANES, V_DIM)
    vn = jnp.pad(v_bf.reshape(db, t, N_HEADS, V_DIM), pad).reshape(db, LANES, V_DIM)

    def page_spec(i):
        return pl.BlockSpec((None, None, page * N_HEADS, V_DIM),
                            lambda b, c, pt: (layer, pt[b, c * npg + i], 0, 0))

    vec = pl.BlockSpec((1, HEAD_DIM), lambda b, c, pt: (0, 0))
    per_seq = lambda rows: pl.BlockSpec((None, rows, V_DIM), lambda b, c, pt: (b, 0, 0))
    in_specs = [vec, vec, vec, vec, pl.BlockSpec((1, V_DIM), lambda b, c, pt: (0, 0)),
                per_seq(nrows), per_seq(LANES), per_seq(LANES)]
    in_specs += [page_spec(i) for i in range(npg)] * 2
    body = functools.partial(_sattn_body, npg=npg, t_pad=t_pad, lam_init=lam_init, scale=HEAD_DIM ** -0.5)
    out = pl.pallas_call(
        body,
        out_shape=jax.ShapeDtypeStruct((db, nrows, V_DIM), F32),
        grid_spec=pltpu.PrefetchScalarGridSpec(
            num_scalar_prefetch=1,
            grid=(db, n_pages // npg),
            in_specs=in_specs,
            out_specs=per_seq(nrows),
            scratch_shapes=[pltpu.VMEM((nrows, 1), F32), pltpu.VMEM((nrows, 1), F32),
                            pltpu.VMEM((nrows, V_DIM), F32)]),
        compiler_params=_params(("arbitrary", "arbitrary"), 48),
        name="attn_sample",
    )(page_table, *lam_vecs, subln_w.reshape(1, V_DIM), qbd, kn, vn, *([ck] * npg), *([cv] * npg))
    out = out.reshape(db, N_HEADS, 2, t_pad, V_DIM)[:, :, 0, :t]
    return out.transpose(0, 2, 1, 3).reshape(db * t, ATTN_W)


def _xattn_prompt_body(q_ref, k_ref, v_ref, o_ref, *, scale):
    s = _dot_nt(q_ref[...], k_ref[...]) * scale
    m = jnp.max(s, axis=-1, keepdims=True)
    p = jnp.exp(s - m)
    p = p / jnp.sum(p, axis=-1, keepdims=True)
    o_ref[...] = _dot(p.astype(BF16), v_ref[...]).astype(o_ref.dtype)


def _xattn_prompt(xq, mk, mv, b, s, n_mem):
    tq = _row_tile(s, 1024)
    nq = s // tq
    body = functools.partial(_xattn_prompt_body, scale=X_HEAD_DIM ** -0.5)
    return pl.pallas_call(
        body,
        out_shape=jax.ShapeDtypeStruct((b * s, X_W), BF16),
        grid=(b, X_HEADS, nq),
        in_specs=[pl.BlockSpec((tq, X_HEAD_DIM), lambda bi, h, i: (bi * nq + i, h)),
                  pl.BlockSpec((n_mem, X_HEAD_DIM), lambda bi, h, i: (bi, h)),
                  pl.BlockSpec((n_mem, X_HEAD_DIM), lambda bi, h, i: (bi, h))],
        out_specs=pl.BlockSpec((tq, X_HEAD_DIM), lambda bi, h, i: (bi * nq + i, h)),
        compiler_params=_params(("parallel", "parallel", "parallel"), 32),
        name="xattn_prompt",
    )(xq, mk, mv)


def _xattn_sample_body(q_ref, k_ref, v_ref, o_ref, *, t, scale):
    g, n_mem = k_ref.shape[0], k_ref.shape[1]
    rows = q_ref.shape[0]
    ncols = g * n_mem * X_HEADS
    kb = k_ref[...].reshape(ncols, X_HEAD_DIM).astype(BF16)
    vb = v_ref[...].reshape(ncols, X_HEAD_DIM).astype(BF16)
    row = lax.broadcasted_iota(jnp.int32, (rows, ncols), 0)
    col = lax.broadcasted_iota(jnp.int32, (rows, ncols), 1)
    same = ((row // (t * X_HEADS)) == (col // (n_mem * X_HEADS))) & ((row % X_HEADS) == (col % X_HEADS))
    s = jnp.where(same, _dot_nt(q_ref[...], kb) * scale, NEG)
    m = jnp.max(s, axis=-1, keepdims=True)
    p = jnp.exp(s - m)
    p = p / jnp.sum(p, axis=-1, keepdims=True)
    o_ref[...] = _dot(p.astype(BF16), vb).astype(o_ref.dtype)


def _xattn_sample(xq, mem_k, mem_v, layer, db, t):
    n_mem = mem_k.shape[2]
    g = math.gcd(XATTN_GROUP, db)
    rows = g * t * X_HEADS
    q = xq.reshape(db * t * X_HEADS, X_HEAD_DIM)
    mem_spec = pl.BlockSpec((None, g, n_mem, X_HEADS, X_HEAD_DIM), lambda i: (layer, i, 0, 0, 0))
    body = functools.partial(_xattn_sample_body, t=t, scale=X_HEAD_DIM ** -0.5)
    out = pl.pallas_call(
        body,
        out_shape=jax.ShapeDtypeStruct((db * t * X_HEADS, X_HEAD_DIM), BF16),
        grid=(db // g,),
        in_specs=[pl.BlockSpec((rows, X_HEAD_DIM), lambda i: (i, 0)), mem_spec, mem_spec],
        out_specs=pl.BlockSpec((rows, X_HEAD_DIM), lambda i: (i, 0)),
        compiler_params=_params(("parallel",), 40),
        name="xattn_sample",
    )(q, mem_k, mem_v)
    return out.reshape(db * t, X_W)


POOL_HALO = 16


def _pool_prompt_body(x_ref, halo_ref, pw_ref, ps_ref, o_ref, buf, *, tm, tiles_per_seq):
    i = pl.program_id(0)
    first = (i % tiles_per_seq) == 0
    x = x_ref[...]
    buf[0:POOL_HALO, :] = jnp.where(first, 0.0, halo_ref[...])
    buf[POOL_HALO:, :] = x
    pos = (i % tiles_per_seq) * tm + lax.broadcasted_iota(jnp.int32, (tm, 1), 0)
    for g, w in enumerate(POOL_WINDOWS):
        cols = slice(g * POOL_GC, (g + 1) * POOL_GC)
        xs = x[:, cols]
        acc = xs
        for j in range(1, w):
            acc = acc + buf[POOL_HALO - j:POOL_HALO - j + tm, cols]
        cnt = jnp.minimum(w, pos + 1).astype(F32)
        d = acc / cnt - xs
        o_ref[:, cols] = (_dot(d.astype(BF16), pw_ref[g]) * ps_ref[:, cols]).astype(o_ref.dtype)


def _pool_prompt(pz, pool_w, pool_scale, s):
    m = pz.shape[0]
    tm = _row_tile(s, 512)
    hb = tm // POOL_HALO
    body = functools.partial(_pool_prompt_body, tm=tm, tiles_per_seq=s // tm)
    return pl.pallas_call(
        body,
        out_shape=jax.ShapeDtypeStruct((m, POOL_W), BF16),
        grid=(m // tm,),
        in_specs=[pl.BlockSpec((tm, POOL_W), lambda i: (i, 0)),
                  pl.BlockSpec((POOL_HALO, POOL_W), lambda i: (jnp.maximum(i * hb - 1, 0), 0)),
                  pl.BlockSpec(pool_w.shape, lambda i: (0, 0, 0)),
                  pl.BlockSpec((1, POOL_W), lambda i: (0, 0))],
        out_specs=pl.BlockSpec((tm, POOL_W), lambda i: (i, 0)),
        scratch_shapes=[pltpu.VMEM((tm + POOL_HALO, POOL_W), F32)],
        compiler_params=_params(("parallel",), 32),
        name="pool_prompt",
    )(pz, pz, pool_w, pool_scale.reshape(1, POOL_W))


def _pool_sample_body(st_ref, pz_ref, pw_ref, ps_ref, o_ref, *, n_state, t, past_len):
    def row(idx, cols):
        return st_ref[idx][:, cols] if idx < n_state else pz_ref[idx - n_state][:, cols]

    for g, w in enumerate(POOL_WINDOWS):
        cols = slice(g * POOL_GC, (g + 1) * POOL_GC)
        ds = []
        for ti in range(t):
            hi = n_state + ti + 1
            lo = max(hi - w, 0)
            acc = row(hi - 1, cols)
            for idx in range(hi - 2, lo - 1, -1):
                acc = acc + row(idx, cols)
            cnt = float(min(w, past_len + ti + 1))
            ds.append(acc / cnt - row(hi - 1, cols))
        d = jnp.concatenate(ds, axis=0)
        o_ref[:, cols] = (_dot(d.astype(BF16), pw_ref[g]) * ps_ref[:, cols]).astype(o_ref.dtype)


def _pool_sample(pz, state_pool, pool_w, pool_scale, db, t, past_len):
    n_state = state_pool.shape[1]
    st = state_pool.transpose(1, 0, 2)
    pzt = pz.reshape(db, t, POOL_W).transpose(1, 0, 2)
    body = functools.partial(_pool_sample_body, n_state=n_state, t=t, past_len=past_len)
    out = pl.pallas_call(
        body,
        out_shape=jax.ShapeDtypeStruct((t * db, POOL_W), BF16),
        grid=(1,),
        in_specs=[pl.BlockSpec(st.shape, lambda i: (0, 0, 0)),
                  pl.BlockSpec(pzt.shape, lambda i: (0, 0, 0)),
                  pl.BlockSpec(pool_w.shape, lambda i: (0, 0, 0)),
                  pl.BlockSpec((1, POOL_W), lambda i: (0, 0))],
        out_specs=pl.BlockSpec((t * db, POOL_W), lambda i: (0, 0)),
        compiler_params=_params(("arbitrary",), 40),
        name="pool_sample",
    )(st, pzt, pool_w, pool_scale.reshape(1, POOL_W))
    return out.reshape(t, db, POOL_W).transpose(1, 0, 2).reshape(db * t, POOL_W)


def _merge_body(a_ref, p_ref, c_ref, wa_ref, wp_ref, wc_ref, g0_ref, g1_ref, g2_ref, o_ref):
    m = (g0_ref[...] * _dot(a_ref[...], wa_ref[...])
         + g1_ref[...] * _dot(p_ref[...], wp_ref[...])
         + g2_ref[...] * _dot(c_ref[...], wc_ref[...]))
    o_ref[...] = m.astype(o_ref.dtype)


def _merge(a, po, c, gates, wa, wp, wc, d, tm):
    m = a.shape[0]
    assert m % tm == 0
    tn = 512
    nj = d // tn
    row = lambda k: pl.BlockSpec((tm, k), lambda i, j: (i, 0))
    wcol = lambda k: pl.BlockSpec((k, tn), lambda i, j: (0, j))
    gate = lambda r: pl.BlockSpec((tm, tn), lambda i, j: (i, r * nj + j))
    return pl.pallas_call(
        _merge_body,
        out_shape=jax.ShapeDtypeStruct((m, d), BF16),
        grid=(m // tm, nj),
        in_specs=[row(ATTN_W), row(POOL_W), row(X_W), wcol(ATTN_W), wcol(POOL_W), wcol(X_W),
                  gate(0), gate(1), gate(2)],
        out_specs=pl.BlockSpec((tm, tn), lambda i, j: (i, j)),
        compiler_params=_params(("parallel", "arbitrary"), 48),
        name="merge",
    )(a, po, c, wa, wp, wc, gates, gates, gates)


def _resid_body(x_ref, w_ref, r_ref, o_ref):
    o_ref[...] = r_ref[...] + _dot(x_ref[...], w_ref[...])


def _resid_proj(x, w, resid, name, *, tm, tn):
    m, k = x.shape
    n = w.shape[1]
    assert m % tm == 0 and n % tn == 0
    return pl.pallas_call(
        _resid_body,
        out_shape=jax.ShapeDtypeStruct((m, n), F32),
        grid=(m // tm, n // tn),
        in_specs=[pl.BlockSpec((tm, k), lambda i, j: (i, 0)),
                  pl.BlockSpec((k, tn), lambda i, j: (0, j)),
                  pl.BlockSpec((tm, tn), lambda i, j: (i, j))],
        out_specs=pl.BlockSpec((tm, tn), lambda i, j: (i, j)),
        compiler_params=_params(("parallel", "arbitrary"), STREAM_VMEM_MIB),
        name=name,
    )(x, w, resid)


CONV_HALO = SUBLANES


def _ffn_a_body(*refs, ni, nj, tm, tiles_per_seq, seq_len, has_prev, cast, emit, n_side):
    h_ref, wg_ref, wu_ref, cw_ref, cb_ref = refs[:5]
    idx = 5
    if has_prev:
        p1_ref, p2_ref = refs[idx:idx + 2]
        idx += 2
    side_in = refs[idx:idx + n_side]
    idx += n_side
    act_ref, g_ref = refs[idx:idx + 2]
    idx += 2
    side_out = refs[idx + 2 * int(emit):idx + 2 * int(emit) + n_side]
    scratch = refs[idx + 2 * int(emit) + n_side:]
    s = pl.program_id(0)
    for src, dst in zip(side_in, side_out):
        dst[...] = src[...].astype(BF16)
    if cast:
        wgb, wub, buf, carry = scratch
        cur = _cast_weight_chunk(s, ni, wg_ref, wgb)
        _cast_weight_chunk(s, ni, wu_ref, wub)
        rhs = (lambda: wgb[cur]), (lambda: wub[cur])
        t = s - ni
        if emit:
            @pl.when((s % ni == 0) & (s >= ni) & (s <= nj * ni))
            def _():
                refs[idx][...] = wgb[cur]
                refs[idx + 1][...] = wub[cur]
    else:
        buf, carry = scratch
        rhs = (lambda: wg_ref[...]), (lambda: wu_ref[...])
        t = s

    @pl.when(s == 0)
    def _():
        carry[...] = jnp.zeros(carry.shape, F32)

    @pl.when(t >= 0)
    def _():
        _ffn_a_tile(s, rhs, h_ref, cw_ref, cb_ref, p1_ref if has_prev else None,
                    p2_ref if has_prev else None, act_ref, g_ref, buf, carry, ni=ni, tm=tm,
                    tiles_per_seq=tiles_per_seq, seq_len=seq_len)


def _ffn_a_tile(s, rhs, h_ref, cw_ref, cb_ref, p1_ref, p2_ref, act_ref, g_ref, buf, carry, *,
                ni, tm, tiles_per_seq, seq_len):
    has_prev = p1_ref is not None
    h = h_ref[...]
    g = _dot(h, rhs[0]())
    u = _dot(h, rhs[1]())
    if tiles_per_seq:
        first = ((s % ni) % tiles_per_seq) == 0
        buf[0:CONV_HALO, :] = jnp.where(first, 0.0, carry[...])
    else:
        buf[0:CONV_HALO, :] = jnp.zeros((CONV_HALO, g.shape[1]), F32)
    buf[CONV_HALO:, :] = g
    carry[...] = g[tm - CONV_HALO:, :]
    g1 = buf[CONV_HALO - 1:CONV_HALO - 1 + tm, :]
    g2 = buf[CONV_HALO - 2:CONV_HALO - 2 + tm, :]
    if not tiles_per_seq:
        t = lax.broadcasted_iota(jnp.int32, (tm, 1), 0) % seq_len
        g1 = jnp.where(t >= 1, g1, 0.0)
        g2 = jnp.where(t >= 2, g2, 0.0)
    if has_prev:
        g1 = g1 + p1_ref[...]
        g2 = g2 + p2_ref[...]
    cw = cw_ref[...]
    gc = cb_ref[...] + g2 * cw[0:1, :]
    gc = gc + g1 * cw[1:2, :]
    gc = gc + g * cw[2:3, :]
    act_ref[...] = (jax.nn.silu(gc) * u).astype(act_ref.dtype)
    if tiles_per_seq:
        g_ref[...] = g[tm - CONV_HALO:, :]
    else:
        g_ref[...] = g


def _ffn_a(h2, wg, wu, conv_w, conv_b, seq_len, tm, prev=None, emit=False, side_casts=()):
    m, d = h2.shape
    dff = wg.shape[1]
    tn = 256
    assert dff % tn == 0 and m % tm == 0
    long_seq = seq_len >= tm
    if long_seq:
        assert seq_len % tm == 0 and prev is None
        tiles_per_seq = seq_len // tm
    else:
        assert tm == m and tm % seq_len == 0
        tiles_per_seq = 0
    ni, nj = m // tm, dff // tn
    w_spec, w_scratch, lead = _weight_specs(wg, d, tn, ni, nj, 0)
    assert wu.dtype == wg.dtype and (not emit or lead)
    tile = _tile_maps(ni, nj, lead, 0)
    col = lambda s: (0, tile(s)[1])
    ins = [h2, wg, wu, conv_w, conv_b.reshape(1, dff)]
    in_specs = [pl.BlockSpec((tm, d), lambda s: (tile(s)[0], 0)),
                w_spec,
                w_spec,
                pl.BlockSpec((CONV_W, tn), col),
                pl.BlockSpec((1, tn), col)]
    if prev is not None:
        ins += list(prev)
        in_specs += [pl.BlockSpec((tm, tn), tile)] * 2
    if long_seq:
        g_shape = jax.ShapeDtypeStruct((ni, CONV_HALO, dff), F32)
        g_spec = pl.BlockSpec((None, CONV_HALO, tn), lambda s: (tile(s)[0], 0, tile(s)[1]))
    else:
        g_shape = jax.ShapeDtypeStruct((m, dff), F32)
        g_spec = pl.BlockSpec((tm, tn), tile)
    body = functools.partial(_ffn_a_body, ni=ni, nj=nj, tm=tm, tiles_per_seq=tiles_per_seq, seq_len=seq_len,
                             has_prev=prev is not None, cast=bool(lead), emit=emit, n_side=len(side_casts))
    out_shape = [jax.ShapeDtypeStruct((m, dff), BF16), g_shape]
    out_specs = [pl.BlockSpec((tm, tn), tile), g_spec]
    if emit:
        out_shape += [jax.ShapeDtypeStruct((d, dff), BF16)] * 2
        out_specs += [_emit_spec(d, tn, ni, nj)] * 2
    n_steps = ni * nj + lead
    side_in, side_out, side_shape = _side_cast_specs(side_casts, n_steps)
    ins += list(side_casts)
    in_specs += side_in
    out_specs += side_out
    out_shape += side_shape
    return pl.pallas_call(
        body,
        out_shape=out_shape,
        grid=(n_steps,),
        in_specs=in_specs,
        out_specs=out_specs,
        scratch_shapes=w_scratch + w_scratch + [pltpu.VMEM((tm + CONV_HALO, tn), F32),
                                                pltpu.VMEM((CONV_HALO, tn), F32)],
        compiler_params=_params(("arbitrary",), STREAM_VMEM_MIB),
        name="ffn_gate_up",
    )(*ins)


def _rope_tables(pos):
    half = HEAD_DIM // 2
    inv = ROPE_THETA ** (-jnp.arange(half, dtype=F32) / half)
    ang = pos.astype(F32)[:, None] * inv[None, :]
    cos, sin = jnp.cos(ang), jnp.sin(ang)
    return jnp.concatenate([cos, cos], axis=-1), jnp.concatenate([-sin, sin], axis=-1)


def _layer(x2, w, lam_init, *, tm, seq_len, rope, attn_fn, pool_fn, xattn_fn, conv_prev, q_dtype, wbf=None):
    d = x2.shape[1]
    rope_rep = rope[0].shape[0] // tm
    pool_off = 3 * ATTN_W
    xq_off = pool_off + POOL_W
    gate_off = xq_off + X_W
    emit = wbf is None
    new_wbf = {}

    def in_proj(key, col0, ncols, out_dtypes, side=(), **kw):
        if emit:
            res = _proj(h, w['w_in'], col0, ncols, out_dtypes, "proj_" + key, tm=tm, emit=True,
                        side_casts=[w[name] for name in side], **kw)
            new_wbf[key] = res[len(out_dtypes)]
            new_wbf.update(zip(side, res[len(out_dtypes) + 1:]))
            return res[:len(out_dtypes)]
        return _proj(h, wbf[key], 0, ncols, out_dtypes, "proj_" + key, tm=tm, **kw)

    h = _rmsnorm(x2, w['attn_norm_w'], "attn_norm")
    (q,) = in_proj('q', 0, ATTN_W, [q_dtype], norm_w=w['q_norm_w'], rope=rope, rope_rep=rope_rep)
    k32, kbf = in_proj('k', ATTN_W, ATTN_W, [F32, BF16], norm_w=w['k_norm_w'], rope=rope, rope_rep=rope_rep)
    v32, vbf = in_proj('v', 2 * ATTN_W, ATTN_W, [F32, BF16])
    (pz,) = in_proj('pool', pool_off, POOL_W, [F32])
    (xq,) = in_proj('xq', xq_off, X_W, [BF16], norm_w=w['xq_norm_w'])
    (gates,) = in_proj('gates', gate_off, 3 * d, [F32], sigmoid=True,
                       side=('w_out', 'w_branch_attn', 'w_branch_pool', 'w_branch_cross'))
    a = attn_fn(q, kbf, vbf)
    po = pool_fn(pz)
    c = xattn_fn(xq)
    if emit:
        act_w = new_wbf
    else:
        act_w = wbf
    m = _merge(a, po, c, gates, act_w['w_branch_attn'], act_w['w_branch_pool'], act_w['w_branch_cross'], d, tm)
    x1 = _resid_proj(m, act_w['w_out'], x2, "out_proj", tm=tm, tn=1024)
    h2 = _rmsnorm(x1, w['ffn_norm_w'], "ffn_norm")
    if emit:
        act, gl, new_wbf['gate'], new_wbf['up'], new_wbf['w_down'] = _ffn_a(
            h2, w['w_gate'], w['w_up'], w['conv_w'], w['conv_b'], seq_len, tm, conv_prev, emit=True,
            side_casts=[w['w_down']])
    else:
        act, gl = _ffn_a(h2, wbf['gate'], wbf['up'], w['conv_w'], w['conv_b'], seq_len, tm, conv_prev)
    y = _resid_proj(act, act_w['w_down'], x1, "down_proj", tm=min(tm, 512), tn=512)
    return y, k32, v32, pz, gl, new_wbf


def kernel(x_prompt, x_sample, cache_k, cache_v, cache_mem_k, cache_mem_v, state_pool, state_conv, page_table, mem_prompt, attn_norm_w, w_in, q_norm_w, k_norm_w, lambda_q1, lambda_k1, lambda_q2, lambda_k2, subln_w, pool_w, pool_scale, mem_norm_w, w_mem_k, w_mem_v, xq_norm_w, xk_norm_w, w_branch_attn, w_branch_pool, w_branch_cross, w_out, ffn_norm_w, w_gate, w_up, conv_w, conv_b, w_down):
    b, s, d = x_prompt.shape
    db, t, _ = x_sample.shape
    depth = w_in.shape[0]
    dff = w_gate.shape[-1]
    n_mem = mem_prompt.shape[1]
    past_len = page_table.shape[1] * cache_k.shape[2]
    n_state = state_pool.shape[2]
    rope_p = _rope_tables(jnp.arange(s, dtype=jnp.int32))
    rope_s = _rope_tables(jnp.tile(past_len + jnp.arange(t, dtype=jnp.int32), db))
    yp = x_prompt.reshape(b * s, d)
    ys = x_sample.reshape(db * t, d)
    outs = [[] for _ in range(10)]
    for l in range(depth):
        lam_init = 0.8 - 0.6 * math.exp(-0.3 * l)
        w = {'attn_norm_w': attn_norm_w[l], 'w_in': w_in[l], 'q_norm_w': q_norm_w[l],
             'k_norm_w': k_norm_w[l], 'xq_norm_w': xq_norm_w[l],
             'w_branch_attn': w_branch_attn[l], 'w_branch_pool': w_branch_pool[l],
             'w_branch_cross': w_branch_cross[l], 'w_out': w_out[l],
             'ffn_norm_w': ffn_norm_w[l], 'w_gate': w_gate[l], 'w_up': w_up[l],
             'conv_w': conv_w[l], 'conv_b': conv_b[l], 'w_down': w_down[l]}
        lam_vecs = [v[l].reshape(1, HEAD_DIM).astype(F32) for v in (lambda_q1, lambda_k1, lambda_q2, lambda_k2)]
        pw = pool_w[l].astype(BF16)
        ps = pool_scale[l]
        sw = subln_w[l]

        mn = _rmsnorm(mem_prompt.reshape(b * n_mem, d), mem_norm_w[l], "mem_norm")
        tm_m = _row_tile(b * n_mem, 1024)
        mk32, mkbf = _proj(mn, w_mem_k[l], 0, X_W, [F32, BF16], "proj_mem_k", tm=tm_m,
                           norm_w=xk_norm_w[l])
        mv32, mvbf = _proj(mn, w_mem_v[l], 0, X_W, [F32, BF16], "proj_mem_v", tm=tm_m)
        tm_p = _row_tile(s, 1024)
        yp, k_p, v_p, pz_p, gl_p, wbf = _layer(
            yp, w, lam_init, tm=tm_p, seq_len=s, rope=rope_p,
            attn_fn=lambda q, k, v: _attn_prompt(q, k, v, lam_vecs, sw, b, s, lam_init),
            pool_fn=lambda pz: _pool_prompt(pz, pw, ps, s),
            xattn_fn=lambda xq: _xattn_prompt(xq, mkbf, mvbf, b, s, n_mem),
            conv_prev=None, q_dtype=BF16)
        tiles_per_seq = s // tm_p
        conv_p = gl_p.reshape(b, tiles_per_seq, CONV_HALO, dff)[:, -1, CONV_HALO - (CONV_W - 1):]

        sc = state_conv[l]
        zrow = jnp.zeros((db, 1, dff), F32)
        p1 = jnp.concatenate([sc[:, 1:2]] + [zrow] * (t - 1), axis=1).reshape(db * t, dff)
        p2 = jnp.concatenate([sc[:, 0:1], sc[:, 1:2]] + [zrow] * (t - 2), axis=1).reshape(db * t, dff)
        sp = state_pool[l]
        ys, k_s, v_s, pz_s, g_s, _ = _layer(
            ys, w, lam_init, tm=_row_tile(db * t, 1024), seq_len=t, rope=rope_s,
            attn_fn=lambda q, k, v: _attn_sample(q, k, v, cache_k, cache_v, l, page_table, lam_vecs, sw, db, t,
                                                 lam_init).astype(BF16),
            pool_fn=lambda pz: _pool_sample(pz, sp, pw, ps, db, t, past_len),
            xattn_fn=lambda xq: _xattn_sample(xq, cache_mem_k, cache_mem_v, l, db, t),
            conv_prev=(p1, p2), q_dtype=F32, wbf=wbf)
        pool_s = jnp.concatenate([sp, pz_s.reshape(db, t, POOL_W)], axis=1)[:, -n_state:]
        conv_s = jnp.concatenate([sc, g_s.reshape(db, t, dff)], axis=1)[:, -(CONV_W - 1):]

        vals = [k_p.reshape(b, s, N_HEADS, V_DIM), v_p.reshape(b, s, N_HEADS, V_DIM),
                mk32.reshape(b, n_mem, X_HEADS, X_HEAD_DIM), mv32.reshape(b, n_mem, X_HEADS, X_HEAD_DIM),
                pz_p.reshape(b, s, POOL_W)[:, -n_state:], conv_p,
                k_s.reshape(db, t, N_HEADS, V_DIM), v_s.reshape(db, t, N_HEADS, V_DIM), pool_s, conv_s]
        for o, v in zip(outs, vals):
            o.append(v)
    return (yp.reshape(b, s, d), ys.reshape(db, t, d)) + tuple(jnp.stack(o) for o in outs)
```

```python
import functools
import math

import jax
import jax.numpy as jnp
from jax import lax
from jax.experimental import pallas as pl
from jax.experimental.pallas import tpu as pltpu

F32 = jnp.float32
BF16 = jnp.bfloat16

N_HEADS = 8
HEAD_DIM = 128
V_DIM = 2 * HEAD_DIM
ATTN_W = N_HEADS * V_DIM
POOL_WINDOWS = (2, 4, 8, 16)
POOL_GC = 256
POOL_W = len(POOL_WINDOWS) * POOL_GC
X_HEADS = 4
X_HEAD_DIM = 256
X_W = X_HEADS * X_HEAD_DIM
CONV_W = 3
ROPE_THETA = 10000.0
EPS = 1e-6
NEG = -1e30
LOG2_E = math.log2(math.e)

SUBLANES = 8
BF16_ROWS = 16
LANES = 128
PAGES_PER_STEP = 8
XATTN_GROUP = 4
MIB = 2 ** 20
STREAM_VMEM_MIB = 56


def _params(sem, vmem_mib):
    return pltpu.CompilerParams(dimension_semantics=sem, vmem_limit_bytes=vmem_mib * MIB)


def _row_tile(m, cap):
    t = min(m, cap)
    assert m % t == 0, (m, t)
    return t


def _dot(a, b):
    return jnp.dot(a, b, preferred_element_type=F32)


def _dot_nt(a, b):
    return lax.dot_general(a, b, (((1,), (1,)), ((), ())), preferred_element_type=F32)


def _rmsnorm_body(x_ref, w_ref, o_ref):
    x = x_ref[...]
    ms = jnp.mean(x * x, axis=-1, keepdims=True)
    o_ref[...] = (x * lax.rsqrt(ms + EPS) * w_ref[...]).astype(o_ref.dtype)


def _rmsnorm(x, w, name):
    m, d = x.shape
    tm = _row_tile(m, 256)
    return pl.pallas_call(
        _rmsnorm_body,
        out_shape=jax.ShapeDtypeStruct((m, d), BF16),
        grid=(m // tm,),
        in_specs=[pl.BlockSpec((tm, d), lambda i: (i, 0)), pl.BlockSpec((1, d), lambda i: (0, 0))],
        out_specs=pl.BlockSpec((tm, d), lambda i: (i, 0)),
        compiler_params=_params(("parallel",), 32),
        name=name,
    )(x, w.reshape(1, d))


def _tile_maps(ni, nj, lead, lag):
    def index(s):
        t = jnp.clip(s - lead - lag, 0, ni * nj - 1)
        return t % ni, t // ni

    return index


def _cast_weight_chunk(s, ni, w_ref, wbf):
    kc = w_ref.shape[0]
    slot = (s // ni) % 2
    r0 = pl.multiple_of((s % ni) * kc, kc)
    wbf[slot, pl.ds(r0, kc), :] = w_ref[...].astype(BF16)
    return 1 - slot


def _weight_specs(w, k, tn, ni, nj, cb0):
    if w.dtype == BF16:
        return pl.BlockSpec((k, tn), lambda s: (0, cb0 + jnp.minimum(s // ni, nj - 1))), [], 0
    assert k % (ni * BF16_ROWS) == 0
    spec = pl.BlockSpec((k // ni, tn), lambda s: (s % ni, cb0 + jnp.minimum(s // ni, nj - 1)))
    return spec, [pltpu.VMEM((2, k, tn), BF16)], ni


def _emit_spec(k, tn, ni, nj):
    return pl.BlockSpec((k, tn), lambda s: (0, jnp.clip(s // ni - 1, 0, nj - 1)))


def _side_cast_specs(arrays, n_steps):
    in_specs, out_specs, out_shape = [], [], []
    for a in arrays:
        rows, cols = a.shape
        rb = BF16_ROWS
        while rows % rb or rows // rb > n_steps:
            rb += BF16_ROWS
        index = lambda s, last=rows // rb - 1: (jnp.minimum(s, last), 0)
        in_specs.append(pl.BlockSpec((rb, cols), index))
        out_specs.append(pl.BlockSpec((rb, cols), index))
        out_shape.append(jax.ShapeDtypeStruct((rows, cols), BF16))
    return in_specs, out_specs, out_shape


def _proj_body(*refs, ni, nj, tn, chunk, rotary, sigmoid, n_out, delayed, cast, emit, n_side):
    x_ref, w_ref = refs[0], refs[1]
    idx = 2
    nw_ref = cos_ref = sin_ref = None
    if chunk:
        nw_ref = refs[idx]
        idx += 1
    if rotary:
        cos_ref, sin_ref = refs[idx], refs[idx + 1]
        idx += 2
    side_in = refs[idx:idx + n_side]
    idx += n_side
    outs = refs[idx:idx + n_out]
    idx += n_out
    side_out = refs[idx + int(emit):idx + int(emit) + n_side]
    scratch = refs[idx + int(emit) + n_side:]
    s = pl.program_id(0)
    for src, dst in zip(side_in, side_out):
        dst[...] = src[...].astype(BF16)
    if cast:
        wbf = scratch[0]
        scratch = scratch[1:]
        cur = _cast_weight_chunk(s, ni, w_ref, wbf)
        rhs = lambda: wbf[cur]
        t = s - ni
        if emit:
            @pl.when((s % ni == 0) & (s >= ni) & (s <= nj * ni))
            def _():
                refs[idx][...] = wbf[cur]
    else:
        rhs = lambda: w_ref[...]
        t = s

    def epilogue(z):
        if sigmoid:
            z = jax.nn.sigmoid(z)
        if not chunk:
            for o in outs:
                o[...] = z.astype(o.dtype)
            return
        nw = nw_ref[...]
        for c in range(tn // chunk):
            cols = slice(c * chunk, (c + 1) * chunk)
            zc = z[:, cols]
            ms = jnp.mean(zc * zc, axis=-1, keepdims=True)
            y = zc * lax.rsqrt(ms + EPS) * nw
            if rotary:
                y = y * cos_ref[...] + pltpu.roll(y, chunk // 2, 1) * sin_ref[...]
            for o in outs:
                o[:, cols] = y.astype(o.dtype)

    if not delayed:
        @pl.when(t >= 0)
        def _():
            epilogue(_dot(x_ref[...], rhs()))
        return

    za, zb = scratch

    @pl.when(t == 0)
    def _():
        zb[...] = jnp.zeros(zb.shape, F32)

    def step(z_w, z_r):
        z_w[...] = _dot(x_ref[...], rhs())
        epilogue(z_r[...])

    @pl.when((t >= 0) & (t % 2 == 0))
    def _():
        step(za, zb)

    @pl.when((t >= 0) & (t % 2 == 1))
    def _():
        step(zb, za)


def _proj(x, w, col0, ncols, out_dtypes, name, *, tm, norm_w=None, rope=None, rope_rep=1, sigmoid=False,
          emit=False, side_casts=(), tn=512):
    m, k = x.shape
    assert m % tm == 0 and ncols % tn == 0 and col0 % tn == 0
    cb0 = col0 // tn
    ni, nj = m // tm, ncols // tn
    chunk = 0 if norm_w is None else norm_w.shape[-1]
    delayed = bool(chunk)
    w_spec, scratch, lead = _weight_specs(w, k, tn, ni, nj, cb0)
    assert not emit or lead
    mm = _tile_maps(ni, nj, lead, 0)
    ep = _tile_maps(ni, nj, lead, int(delayed))
    ins = [x, w]
    in_specs = [pl.BlockSpec((tm, k), lambda s: (mm(s)[0], 0)), w_spec]
    if chunk:
        ins.append(norm_w.reshape(1, chunk).astype(F32))
        in_specs.append(pl.BlockSpec((1, chunk), lambda s: (0, 0)))
    if rope is not None:
        for t in rope:
            ins.append(t)
            in_specs.append(pl.BlockSpec((tm, chunk), lambda s: (ep(s)[0] % rope_rep, 0)))
    out_shape = [jax.ShapeDtypeStruct((m, ncols), dt) for dt in out_dtypes]
    out_specs = [pl.BlockSpec((tm, tn), lambda s: ep(s)) for _ in out_dtypes]
    if emit:
        out_shape.append(jax.ShapeDtypeStruct((k, ncols), BF16))
        out_specs.append(_emit_spec(k, tn, ni, nj))
    n_steps = ni * nj + lead + int(delayed)
    side_in, side_out, side_shape = _side_cast_specs(side_casts, n_steps)
    ins += list(side_casts)
    in_specs += side_in
    out_specs += side_out
    out_shape += side_shape
    body = functools.partial(_proj_body, ni=ni, nj=nj, tn=tn, chunk=chunk, rotary=rope is not None,
                             sigmoid=sigmoid, n_out=len(out_dtypes), delayed=delayed, cast=bool(lead), emit=emit,
                             n_side=len(side_casts))
    if delayed:
        scratch = scratch + [pltpu.VMEM((tm, tn), F32), pltpu.VMEM((tm, tn), F32)]
    res = pl.pallas_call(
        body,
        out_shape=out_shape,
        grid=(n_steps,),
        in_specs=in_specs,
        out_specs=out_specs,
        scratch_shapes=scratch,
        compiler_params=_params(("arbitrary",), STREAM_VMEM_MIB),
        name=name,
    )(*ins)
    return res


def _lam_value(lq1, lk1, lq2, lk2, lam_init):
    a = jnp.sum(lq1[...] * lk1[...], axis=-1, keepdims=True)
    b = jnp.sum(lq2[...] * lk2[...], axis=-1, keepdims=True)
    return jnp.exp(a) - jnp.exp(b) + lam_init


def _subln(o, sw, lam_init):
    ms = jnp.mean(o * o, axis=-1, keepdims=True)
    return (o * lax.rsqrt(ms + EPS) * sw) * (1.0 - lam_init)


def _attn_body(lq1, lk1, lq2, lk2, sw_ref, q_ref, k_ref, v_ref, o_ref,
               m1, l1, a1, m2, l2, a2, *, tq, lam_init, scale):
    qi = pl.program_id(2)
    q = q_ref[...]
    qs = (q[:, :HEAD_DIM], q[:, HEAD_DIM:])
    stats = ((m1, l1, a1), (m2, l2, a2))
    for m, l, a in stats:
        m[...] = jnp.full(m.shape, NEG, F32)
        l[...] = jnp.zeros(l.shape, F32)
        a[...] = jnp.zeros(a.shape, F32)

    row = lax.broadcasted_iota(jnp.int32, (tq, tq), 0)
    col = lax.broadcasted_iota(jnp.int32, (tq, tq), 1)
    diag = col <= row

    def scores(j, mask):
        start = pl.multiple_of(j * tq, tq)
        kb = k_ref[pl.ds(start, tq), :]
        out = []
        for c in range(2):
            s = _dot_nt(qs[c], kb[:, c * HEAD_DIM:(c + 1) * HEAD_DIM])
            out.append(s if mask is None else jnp.where(mask, s, NEG))
        return out

    def lane_fold(x, op):
        r = x[:, 0:LANES]
        for i in range(1, tq // LANES):
            r = op(r, x[:, i * LANES:(i + 1) * LANES])
        return r

    def track_max(j, mask):
        for s, (m, l, a) in zip(scores(j, mask), stats):
            m[...] = jnp.maximum(m[...], lane_fold(s, jnp.maximum))

    def max_body(j, carry):
        track_max(j, None)
        return carry

    lax.fori_loop(0, qi, max_body, 0)
    track_max(qi, diag)
    row_max = [jnp.max(m[...], axis=-1, keepdims=True) for m, _, _ in stats]

    def accumulate(j, mask):
        start = pl.multiple_of(j * tq, tq)
        vb = v_ref[pl.ds(start, tq), :]
        for s, mf, (m, l, a) in zip(scores(j, mask), row_max, stats):
            p = jnp.exp2((s - mf) * (scale * LOG2_E))
            l[...] = l[...] + lane_fold(p, jnp.add)
            a[...] = a[...] + _dot(p.astype(BF16), vb)

    def acc_body(j, carry):
        accumulate(j, None)
        return carry

    lax.fori_loop(0, qi, acc_body, 0)
    accumulate(qi, diag)

    lam = _lam_value(lq1, lk1, lq2, lk2, lam_init)
    row_sum = [jnp.sum(l[...], axis=-1, keepdims=True) for _, l, _ in stats]
    o = a1[...] / row_sum[0] - lam * (a2[...] / row_sum[1])
    o_ref[...] = _subln(o, sw_ref[...], lam_init).astype(o_ref.dtype)


def _attn_prompt(q, k, v, lam_vecs, subln_w, b, s, lam_init):
    tq = _row_tile(s, 1024)
    nq = s // tq
    vec = pl.BlockSpec((1, HEAD_DIM), lambda bi, h, i: (0, 0))
    body = functools.partial(_attn_body, tq=tq, lam_init=lam_init, scale=HEAD_DIM ** -0.5)
    return pl.pallas_call(
        body,
        out_shape=jax.ShapeDtypeStruct((b * s, ATTN_W), BF16),
        grid=(b, N_HEADS, nq),
        in_specs=[vec, vec, vec, vec,
                  pl.BlockSpec((1, V_DIM), lambda bi, h, i: (0, 0)),
                  pl.BlockSpec((tq, V_DIM), lambda bi, h, i: (bi * nq + i, h)),
                  pl.BlockSpec((s, V_DIM), lambda bi, h, i: (bi, h)),
                  pl.BlockSpec((s, V_DIM), lambda bi, h, i: (bi, h))],
        out_specs=pl.BlockSpec((tq, V_DIM), lambda bi, h, i: (bi * nq + i, h)),
        scratch_shapes=[pltpu.VMEM((tq, LANES), F32), pltpu.VMEM((tq, LANES), F32), pltpu.VMEM((tq, V_DIM), F32),
                        pltpu.VMEM((tq, LANES), F32), pltpu.VMEM((tq, LANES), F32), pltpu.VMEM((tq, V_DIM), F32)],
        compiler_params=_params(("parallel", "parallel", "arbitrary"), 40),
        name="attn_prompt",
    )(*lam_vecs, subln_w.reshape(1, V_DIM), q, k, v)


def _sattn_body(pt_ref, lq1, lk1, lq2, lk2, sw_ref, q_ref, kn_ref, vn_ref, *rest, npg, t_pad, lam_init, scale):
    del pt_ref
    k_refs = rest[:npg]
    v_refs = rest[npg:2 * npg]
    o_ref = rest[2 * npg]
    m_ref, l_ref, acc_ref = rest[2 * npg + 1:]
    c = pl.program_id(1)
    q = q_ref[...].astype(BF16)
    nrows = q.shape[0]

    def masks(ncols):
        row = lax.broadcasted_iota(jnp.int32, (nrows, ncols), 0)
        col = lax.broadcasted_iota(jnp.int32, (nrows, ncols), 1)
        same_head = (row // (2 * t_pad)) == (col % N_HEADS)
        return same_head, row % t_pad, col // N_HEADS

    def update(kbs, vbs, mask):
        ss = [jnp.where(mask, _dot_nt(q, kb) * scale, NEG) for kb in kbs]
        m_cur = functools.reduce(jnp.maximum, [jnp.max(s, axis=-1, keepdims=True) for s in ss])
        m_old = m_ref[...]
        m_new = jnp.maximum(m_old, m_cur)
        alpha = jnp.exp(m_old - m_new)
        ps = [jnp.exp(s - m_new) for s in ss]
        l_ref[...] = alpha * l_ref[...] + sum(jnp.sum(p, axis=-1, keepdims=True) for p in ps)
        acc_ref[...] = alpha * acc_ref[...] + sum(_dot(p.astype(BF16), vb) for p, vb in zip(ps, vbs))
        m_ref[...] = m_new

    @pl.when(c == 0)
    def _():
        m_ref[...] = jnp.full(m_ref.shape, NEG, F32)
        l_ref[...] = jnp.zeros(l_ref.shape, F32)
        acc_ref[...] = jnp.zeros(acc_ref.shape, F32)
        same_head, step, tok = masks(kn_ref.shape[0])
        update([kn_ref[...]], [vn_ref[...]], same_head & (tok <= step))

    update([r[...].astype(BF16) for r in k_refs], [r[...].astype(BF16) for r in v_refs],
           masks(k_refs[0].shape[0])[0])

    @pl.when(c == pl.num_programs(1) - 1)
    def _():
        lam = _lam_value(lq1, lk1, lq2, lk2, lam_init)
        y = acc_ref[...] / l_ref[...]
        o = y - lam * pltpu.roll(y, nrows - t_pad, 0)
        o_ref[...] = _subln(o, sw_ref[...], lam_init)


def _attn_sample(q32, k_bf, v_bf, cache_k, cache_v, layer, page_table, lam_vecs, subln_w, db, t, lam_init):
    t_pad = SUBLANES // 2
    assert t <= t_pad
    depth, n_phys, page, _, _ = cache_k.shape
    n_pages = page_table.shape[1]
    npg = math.gcd(PAGES_PER_STEP, n_pages)
    ck = cache_k.reshape(depth, n_phys, page * N_HEADS, V_DIM)
    cv = cache_v.reshape(depth, n_phys, page * N_HEADS, V_DIM)
    q5 = q32.reshape(db, t, N_HEADS, 2, HEAD_DIM).transpose(0, 2, 3, 1, 4)
    q5 = jnp.pad(q5, ((0, 0), (0, 0), (0, 0), (0, t_pad - t), (0, 0)))
    eye = jnp.eye(2, dtype=F32)
    nrows = N_HEADS * 2 * t_pad
    qbd = (q5[:, :, :, :, None, :] * eye[None, None, :, None, :, None]).reshape(db, nrows, V_DIM)
    new_tok = LANES // N_HEADS
    pad = ((0, 0), (0, new_tok - t), (0, 0), (0, 0))
    kn = jnp.pad(k_bf.reshape(db, t, N_HEADS, V_DIM), pad).reshape(db, LANES, V_DIM)
    vn = jnp.pad(v_bf.reshape(db, t, N_HEADS, V_DIM), pad).reshape(db, LANES, V_DIM)

    def page_spec(i):
        return pl.BlockSpec((None, None, page * N_HEADS, V_DIM),
                            lambda b, c, pt: (layer, pt[b, c * npg + i], 0, 0))

    vec = pl.BlockSpec((1, HEAD_DIM), lambda b, c, pt: (0, 0))
    per_seq = lambda rows: pl.BlockSpec((None, rows, V_DIM), lambda b, c, pt: (b, 0, 0))
    in_specs = [vec, vec, vec, vec, pl.BlockSpec((1, V_DIM), lambda b, c, pt: (0, 0)),
                per_seq(nrows), per_seq(LANES), per_seq(LANES)]
    in_specs += [page_spec(i) for i in range(npg)] * 2
    body = functools.partial(_sattn_body, npg=npg, t_pad=t_pad, lam_init=lam_init, scale=HEAD_DIM ** -0.5)
    out = pl.pallas_call(
        body,
        out_shape=jax.ShapeDtypeStruct((db, nrows, V_DIM), F32),
        grid_spec=pltpu.PrefetchScalarGridSpec(
            num_scalar_prefetch=1,
            grid=(db, n_pages // npg),
            in_specs=in_specs,
            out_specs=per_seq(nrows),
            scratch_shapes=[pltpu.VMEM((nrows, 1), F32), pltpu.VMEM((nrows, 1), F32),
                            pltpu.VMEM((nrows, V_DIM), F32)]),
        compiler_params=_params(("arbitrary", "arbitrary"), 48),
        name="attn_sample",
    )(page_table, *lam_vecs, subln_w.reshape(1, V_DIM), qbd, kn, vn, *([ck] * npg), *([cv] * npg))
    out = out.reshape(db, N_HEADS, 2, t_pad, V_DIM)[:, :, 0, :t]
    return out.transpose(0, 2, 1, 3).reshape(db * t, ATTN_W)


def _xattn_prompt_body(q_ref, k_ref, v_ref, o_ref, *, scale):
    s = _dot_nt(q_ref[...], k_ref[...]) * scale
    m = jnp.max(s, axis=-1, keepdims=True)
    p = jnp.exp(s - m)
    p = p / jnp.sum(p, axis=-1, keepdims=True)
    o_ref[...] = _dot(p.astype(BF16), v_ref[...]).astype(o_ref.dtype)


def _xattn_prompt(xq, mk, mv, b, s, n_mem):
    tq = _row_tile(s, 1024)
    nq = s // tq
    body = functools.partial(_xattn_prompt_body, scale=X_HEAD_DIM ** -0.5)
    return pl.pallas_call(
        body,
        out_shape=jax.ShapeDtypeStruct((b * s, X_W), BF16),
        grid=(b, X_HEADS, nq),
        in_specs=[pl.BlockSpec((tq, X_HEAD_DIM), lambda bi, h, i: (bi * nq + i, h)),
                  pl.BlockSpec((n_mem, X_HEAD_DIM), lambda bi, h, i: (bi, h)),
                  pl.BlockSpec((n_mem, X_HEAD_DIM), lambda bi, h, i: (bi, h))],
        out_specs=pl.BlockSpec((tq, X_HEAD_DIM), lambda bi, h, i: (bi * nq + i, h)),
        compiler_params=_params(("parallel", "parallel", "parallel"), 32),
        name="xattn_prompt",
    )(xq, mk, mv)


def _xattn_sample_body(q_ref, k_ref, v_ref, o_ref, *, t, scale):
    g, n_mem = k_ref.shape[0], k_ref.shape[1]
    rows = q_ref.shape[0]
    ncols = g * n_mem * X_HEADS
    kb = k_ref[...].reshape(ncols, X_HEAD_DIM).astype(BF16)
    vb = v_ref[...].reshape(ncols, X_HEAD_DIM).astype(BF16)
    row = lax.broadcasted_iota(jnp.int32, (rows, ncols), 0)
    col = lax.broadcasted_iota(jnp.int32, (rows, ncols), 1)
    same = ((row // (t * X_HEADS)) == (col // (n_mem * X_HEADS))) & ((row % X_HEADS) == (col % X_HEADS))
    s = jnp.where(same, _dot_nt(q_ref[...], kb) * scale, NEG)
    m = jnp.max(s, axis=-1, keepdims=True)
    p = jnp.exp(s - m)
    p = p / jnp.sum(p, axis=-1, keepdims=True)
    o_ref[...] = _dot(p.astype(BF16), vb).astype(o_ref.dtype)


def _xattn_sample(xq, mem_k, mem_v, layer, db, t):
    n_mem = mem_k.shape[2]
    g = math.gcd(XATTN_GROUP, db)
    rows = g * t * X_HEADS
    q = xq.reshape(db * t * X_HEADS, X_HEAD_DIM)
    mem_spec = pl.BlockSpec((None, g, n_mem, X_HEADS, X_HEAD_DIM), lambda i: (layer, i, 0, 0, 0))
    body = functools.partial(_xattn_sample_body, t=t, scale=X_HEAD_DIM ** -0.5)
    out = pl.pallas_call(
        body,
        out_shape=jax.ShapeDtypeStruct((db * t * X_HEADS, X_HEAD_DIM), BF16),
        grid=(db // g,),
        in_specs=[pl.BlockSpec((rows, X_HEAD_DIM), lambda i: (i, 0)), mem_spec, mem_spec],
        out_specs=pl.BlockSpec((rows, X_HEAD_DIM), lambda i: (i, 0)),
        compiler_params=_params(("parallel",), 40),
        name="xattn_sample",
    )(q, mem_k, mem_v)
    return out.reshape(db * t, X_W)


POOL_HALO = 16


def _pool_prompt_body(x_ref, halo_ref, pw_ref, ps_ref, o_ref, buf, *, tm, tiles_per_seq):
    i = pl.program_id(0)
    first = (i % tiles_per_seq) == 0
    x = x_ref[...]
    buf[0:POOL_HALO, :] = jnp.where(first, 0.0, halo_ref[...])
    buf[POOL_HALO:, :] = x
    pos = (i % tiles_per_seq) * tm + lax.broadcasted_iota(jnp.int32, (tm, 1), 0)
    for g, w in enumerate(POOL_WINDOWS):
        cols = slice(g * POOL_GC, (g + 1) * POOL_GC)
        xs = x[:, cols]
        acc = xs
        for j in range(1, w):
            acc = acc + buf[POOL_HALO - j:POOL_HALO - j + tm, cols]
        cnt = jnp.minimum(w, pos + 1).astype(F32)
        d = acc / cnt - xs
        o_ref[:, cols] = (_dot(d.astype(BF16), pw_ref[g]) * ps_ref[:, cols]).astype(o_ref.dtype)


def _pool_prompt(pz, pool_w, pool_scale, s):
    m = pz.shape[0]
    tm = _row_tile(s, 512)
    hb = tm // POOL_HALO
    body = functools.partial(_pool_prompt_body, tm=tm, tiles_per_seq=s // tm)
    return pl.pallas_call(
        body,
        out_shape=jax.ShapeDtypeStruct((m, POOL_W), BF16),
        grid=(m // tm,),
        in_specs=[pl.BlockSpec((tm, POOL_W), lambda i: (i, 0)),
                  pl.BlockSpec((POOL_HALO, POOL_W), lambda i: (jnp.maximum(i * hb - 1, 0), 0)),
                  pl.BlockSpec(pool_w.shape, lambda i: (0, 0, 0)),
                  pl.BlockSpec((1, POOL_W), lambda i: (0, 0))],
        out_specs=pl.BlockSpec((tm, POOL_W), lambda i: (i, 0)),
        scratch_shapes=[pltpu.VMEM((tm + POOL_HALO, POOL_W), F32)],
        compiler_params=_params(("parallel",), 32),
        name="pool_prompt",
    )(pz, pz, pool_w, pool_scale.reshape(1, POOL_W))


def _pool_sample_body(st_ref, pz_ref, pw_ref, ps_ref, o_ref, *, n_state, t, past_len):
    def row(idx, cols):
        return st_ref[idx][:, cols] if idx < n_state else pz_ref[idx - n_state][:, cols]

    for g, w in enumerate(POOL_WINDOWS):
        cols = slice(g * POOL_GC, (g + 1) * POOL_GC)
        ds = []
        for ti in range(t):
            hi = n_state + ti + 1
            lo = max(hi - w, 0)
            acc = row(hi - 1, cols)
            for idx in range(hi - 2, lo - 1, -1):
                acc = acc + row(idx, cols)
            cnt = float(min(w, past_len + ti + 1))
            ds.append(acc / cnt - row(hi - 1, cols))
        d = jnp.concatenate(ds, axis=0)
        o_ref[:, cols] = (_dot(d.astype(BF16), pw_ref[g]) * ps_ref[:, cols]).astype(o_ref.dtype)


def _pool_sample(pz, state_pool, pool_w, pool_scale, db, t, past_len):
    n_state = state_pool.shape[1]
    st = state_pool.transpose(1, 0, 2)
    pzt = pz.reshape(db, t, POOL_W).transpose(1, 0, 2)
    body = functools.partial(_pool_sample_body, n_state=n_state, t=t, past_len=past_len)
    out = pl.pallas_call(
        body,
        out_shape=jax.ShapeDtypeStruct((t * db, POOL_W), BF16),
        grid=(1,),
        in_specs=[pl.BlockSpec(st.shape, lambda i: (0, 0, 0)),
                  pl.BlockSpec(pzt.shape, lambda i: (0, 0, 0)),
                  pl.BlockSpec(pool_w.shape, lambda i: (0, 0, 0)),
                  pl.BlockSpec((1, POOL_W), lambda i: (0, 0))],
        out_specs=pl.BlockSpec((t * db, POOL_W), lambda i: (0, 0)),
        compiler_params=_params(("arbitrary",), 40),
        name="pool_sample",
    )(st, pzt, pool_w, pool_scale.reshape(1, POOL_W))
    return out.reshape(t, db, POOL_W).transpose(1, 0, 2).reshape(db * t, POOL_W)


def _merge_body(a_ref, p_ref, c_ref, wa_ref, wp_ref, wc_ref, g0_ref, g1_ref, g2_ref, o_ref):
    m = (g0_ref[...] * _dot(a_ref[...], wa_ref[...])
         + g1_ref[...] * _dot(p_ref[...], wp_ref[...])
         + g2_ref[...] * _dot(c_ref[...], wc_ref[...]))
    o_ref[...] = m.astype(o_ref.dtype)


def _merge(a, po, c, gates, wa, wp, wc, d, tm):
    m = a.shape[0]
    assert m % tm == 0
    tn = 512
    nj = d // tn
    row = lambda k: pl.BlockSpec((tm, k), lambda i, j: (i, 0))
    wcol = lambda k: pl.BlockSpec((k, tn), lambda i, j: (0, j))
    gate = lambda r: pl.BlockSpec((tm, tn), lambda i, j: (i, r * nj + j))
    return pl.pallas_call(
        _merge_body,
        out_shape=jax.ShapeDtypeStruct((m, d), BF16),
        grid=(m // tm, nj),
        in_specs=[row(ATTN_W), row(POOL_W), row(X_W), wcol(ATTN_W), wcol(POOL_W), wcol(X_W),
                  gate(0), gate(1), gate(2)],
        out_specs=pl.BlockSpec((tm, tn), lambda i, j: (i, j)),
        compiler_params=_params(("parallel", "arbitrary"), 48),
        name="merge",
    )(a, po, c, wa, wp, wc, gates, gates, gates)


def _resid_body(x_ref, w_ref, r_ref, o_ref):
    o_ref[...] = r_ref[...] + _dot(x_ref[...], w_ref[...])


def _resid_proj(x, w, resid, name, *, tm, tn):
    m, k = x.shape
    n = w.shape[1]
    assert m % tm == 0 and n % tn == 0
    return pl.pallas_call(
        _resid_body,
        out_shape=jax.ShapeDtypeStruct((m, n), F32),
        grid=(m // tm, n // tn),
        in_specs=[pl.BlockSpec((tm, k), lambda i, j: (i, 0)),
                  pl.BlockSpec((k, tn), lambda i, j: (0, j)),
                  pl.BlockSpec((tm, tn), lambda i, j: (i, j))],
        out_specs=pl.BlockSpec((tm, tn), lambda i, j: (i, j)),
        compiler_params=_params(("parallel", "arbitrary"), STREAM_VMEM_MIB),
        name=name,
    )(x, w, resid)


CONV_HALO = SUBLANES


def _ffn_a_body(*refs, ni, nj, tm, tiles_per_seq, seq_len, has_prev, cast, emit, n_side):
    h_ref, wg_ref, wu_ref, cw_ref, cb_ref = refs[:5]
    idx = 5
    if has_prev:
        p1_ref, p2_ref = refs[idx:idx + 2]
        idx += 2
    side_in = refs[idx:idx + n_side]
    idx += n_side
    act_ref, g_ref = refs[idx:idx + 2]
    idx += 2
    side_out = refs[idx + 2 * int(emit):idx + 2 * int(emit) + n_side]
    scratch = refs[idx + 2 * int(emit) + n_side:]
    s = pl.program_id(0)
    for src, dst in zip(side_in, side_out):
        dst[...] = src[...].astype(BF16)
    if cast:
        wgb, wub, buf, carry = scratch
        cur = _cast_weight_chunk(s, ni, wg_ref, wgb)
        _cast_weight_chunk(s, ni, wu_ref, wub)
        rhs = (lambda: wgb[cur]), (lambda: wub[cur])
        t = s - ni
        if emit:
            @pl.when((s % ni == 0) & (s >= ni) & (s <= nj * ni))
            def _():
                refs[idx][...] = wgb[cur]
                refs[idx + 1][...] = wub[cur]
    else:
        buf, carry = scratch
        rhs = (lambda: wg_ref[...]), (lambda: wu_ref[...])
        t = s

    @pl.when(s == 0)
    def _():
        carry[...] = jnp.zeros(carry.shape, F32)

    @pl.when(t >= 0)
    def _():
        _ffn_a_tile(s, rhs, h_ref, cw_ref, cb_ref, p1_ref if has_prev else None,
                    p2_ref if has_prev else None, act_ref, g_ref, buf, carry, ni=ni, tm=tm,
                    tiles_per_seq=tiles_per_seq, seq_len=seq_len)


def _ffn_a_tile(s, rhs, h_ref, cw_ref, cb_ref, p1_ref, p2_ref, act_ref, g_ref, buf, carry, *,
                ni, tm, tiles_per_seq, seq_len):
    has_prev = p1_ref is not None
    h = h_ref[...]
    g = _dot(h, rhs[0]())
    u = _dot(h, rhs[1]())
    if tiles_per_seq:
        first = ((s % ni) % tiles_per_seq) == 0
        buf[0:CONV_HALO, :] = jnp.where(first, 0.0, carry[...])
    else:
        buf[0:CONV_HALO, :] = jnp.zeros((CONV_HALO, g.shape[1]), F32)
    buf[CONV_HALO:, :] = g
    carry[...] = g[tm - CONV_HALO:, :]
    g1 = buf[CONV_HALO - 1:CONV_HALO - 1 + tm, :]
    g2 = buf[CONV_HALO - 2:CONV_HALO - 2 + tm, :]
    if not tiles_per_seq:
        t = lax.broadcasted_iota(jnp.int32, (tm, 1), 0) % seq_len
        g1 = jnp.where(t >= 1, g1, 0.0)
        g2 = jnp.where(t >= 2, g2, 0.0)
    if has_prev:
        g1 = g1 + p1_ref[...]
        g2 = g2 + p2_ref[...]
    cw = cw_ref[...]
    gc = cb_ref[...] + g2 * cw[0:1, :]
    gc = gc + g1 * cw[1:2, :]
    gc = gc + g * cw[2:3, :]
    act_ref[...] = (jax.nn.silu(gc) * u).astype(act_ref.dtype)
    if tiles_per_seq:
        g_ref[...] = g[tm - CONV_HALO:, :]
    else:
        g_ref[...] = g


def _ffn_a(h2, wg, wu, conv_w, conv_b, seq_len, tm, prev=None, emit=False, side_casts=()):
    m, d = h2.shape
    dff = wg.shape[1]
    tn = 256
    assert dff % tn == 0 and m % tm == 0
    long_seq = seq_len >= tm
    if long_seq:
        assert seq_len % tm == 0 and prev is None
        tiles_per_seq = seq_len // tm
    else:
        assert tm == m and tm % seq_len == 0
        tiles_per_seq = 0
    ni, nj = m // tm, dff // tn
    w_spec, w_scratch, lead = _weight_specs(wg, d, tn, ni, nj, 0)
    assert wu.dtype == wg.dtype and (not emit or lead)
    tile = _tile_maps(ni, nj, lead, 0)
    col = lambda s: (0, tile(s)[1])
    ins = [h2, wg, wu, conv_w, conv_b.reshape(1, dff)]
    in_specs = [pl.BlockSpec((tm, d), lambda s: (tile(s)[0], 0)),
                w_spec,
                w_spec,
                pl.BlockSpec((CONV_W, tn), col),
                pl.BlockSpec((1, tn), col)]
    if prev is not None:
        ins += list(prev)
        in_specs += [pl.BlockSpec((tm, tn), tile)] * 2
    if long_seq:
        g_shape = jax.ShapeDtypeStruct((ni, CONV_HALO, dff), F32)
        g_spec = pl.BlockSpec((None, CONV_HALO, tn), lambda s: (tile(s)[0], 0, tile(s)[1]))
    else:
        g_shape = jax.ShapeDtypeStruct((m, dff), F32)
        g_spec = pl.BlockSpec((tm, tn), tile)
    body = functools.partial(_ffn_a_body, ni=ni, nj=nj, tm=tm, tiles_per_seq=tiles_per_seq, seq_len=seq_len,
                             has_prev=prev is not None, cast=bool(lead), emit=emit, n_side=len(side_casts))
    out_shape = [jax.ShapeDtypeStruct((m, dff), BF16), g_shape]
    out_specs = [pl.BlockSpec((tm, tn), tile), g_spec]
    if emit:
        out_shape += [jax.ShapeDtypeStruct((d, dff), BF16)] * 2
        out_specs += [_emit_spec(d, tn, ni, nj)] * 2
    n_steps = ni * nj + lead
    side_in, side_out, side_shape = _side_cast_specs(side_casts, n_steps)
    ins += list(side_casts)
    in_specs += side_in
    out_specs += side_out
    out_shape += side_shape
    return pl.pallas_call(
        body,
        out_shape=out_shape,
        grid=(n_steps,),
        in_specs=in_specs,
        out_specs=out_specs,
        scratch_shapes=w_scratch + w_scratch + [pltpu.VMEM((tm + CONV_HALO, tn), F32),
                                                pltpu.VMEM((CONV_HALO, tn), F32)],
        compiler_params=_params(("arbitrary",), STREAM_VMEM_MIB),
        name="ffn_gate_up",
    )(*ins)


def _rope_tables(pos):
    half = HEAD_DIM // 2
    inv = ROPE_THETA ** (-jnp.arange(half, dtype=F32) / half)
    ang = pos.astype(F32)[:, None] * inv[None, :]
    cos, sin = jnp.cos(ang), jnp.sin(ang)
    return jnp.concatenate([cos, cos], axis=-1), jnp.concatenate([-sin, sin], axis=-1)


def _layer(x2, w, lam_init, *, tm, seq_len, rope, attn_fn, pool_fn, xattn_fn, conv_prev, q_dtype, wbf=None):
    d = x2.shape[1]
    rope_rep = rope[0].shape[0] // tm
    pool_off = 3 * ATTN_W
    xq_off = pool_off + POOL_W
    gate_off = xq_off + X_W
    emit = wbf is None
    new_wbf = {}

    def in_proj(key, col0, ncols, out_dtypes, side=(), **kw):
        if emit:
            res = _proj(h, w['w_in'], col0, ncols, out_dtypes, "proj_" + key, tm=tm, emit=True,
                        side_casts=[w[name] for name in side], **kw)
            new_wbf[key] = res[len(out_dtypes)]
            new_wbf.update(zip(side, res[len(out_dtypes) + 1:]))
            return res[:len(out_dtypes)]
        return _proj(h, wbf[key], 0, ncols, out_dtypes, "proj_" + key, tm=tm, **kw)

    h = _rmsnorm(x2, w['attn_norm_w'], "attn_norm")
    (q,) = in_proj('q', 0, ATTN_W, [q_dtype], norm_w=w['q_norm_w'], rope=rope, rope_rep=rope_rep)
    k32, kbf = in_proj('k', ATTN_W, ATTN_W, [F32, BF16], norm_w=w['k_norm_w'], rope=rope, rope_rep=rope_rep)
    v32, vbf = in_proj('v', 2 * ATTN_W, ATTN_W, [F32, BF16])
    (pz,) = in_proj('pool', pool_off, POOL_W, [F32])
    (xq,) = in_proj('xq', xq_off, X_W, [BF16], norm_w=w['xq_norm_w'])
    (gates,) = in_proj('gates', gate_off, 3 * d, [F32], sigmoid=True,
                       side=('w_out', 'w_branch_attn', 'w_branch_pool', 'w_branch_cross'))
    a = attn_fn(q, kbf, vbf)
    po = pool_fn(pz)
    c = xattn_fn(xq)
    if emit:
        act_w = new_wbf
    else:
        act_w = wbf
    m = _merge(a, po, c, gates, act_w['w_branch_attn'], act_w['w_branch_pool'], act_w['w_branch_cross'], d, tm)
    x1 = _resid_proj(m, act_w['w_out'], x2, "out_proj", tm=tm, tn=1024)
    h2 = _rmsnorm(x1, w['ffn_norm_w'], "ffn_norm")
    if emit:
        act, gl, new_wbf['gate'], new_wbf['up'], new_wbf['w_down'] = _ffn_a(
            h2, w['w_gate'], w['w_up'], w['conv_w'], w['conv_b'], seq_len, tm, conv_prev, emit=True,
            side_casts=[w['w_down']])
    else:
        act, gl = _ffn_a(h2, wbf['gate'], wbf['up'], w['conv_w'], w['conv_b'], seq_len, tm, conv_prev)
    y = _resid_proj(act, act_w['w_down'], x1, "down_proj", tm=min(tm, 512), tn=512)
    return y, k32, v32, pz, gl, new_wbf


def kernel(x_prompt, x_sample, cache_k, cache_v, cache_mem_k, cache_mem_v, state_pool, state_conv, page_table, mem_prompt, attn_norm_w, w_in, q_norm_w, k_norm_w, lambda_q1, lambda_k1, lambda_q2, lambda_k2, subln_w, pool_w, pool_scale, mem_norm_w, w_mem_k, w_mem_v, xq_norm_w, xk_norm_w, w_branch_attn, w_branch_pool, w_branch_cross, w_out, ffn_norm_w, w_gate, w_up, conv_w, conv_b, w_down):
    b, s, d = x_prompt.shape
    db, t, _ = x_sample.shape
    depth = w_in.shape[0]
    dff = w_gate.shape[-1]
    n_mem = mem_prompt.shape[1]
    past_len = page_table.shape[1] * cache_k.shape[2]
    n_state = state_pool.shape[2]
    rope_p = _rope_tables(jnp.arange(s, dtype=jnp.int32))
    rope_s = _rope_tables(jnp.tile(past_len + jnp.arange(t, dtype=jnp.int32), db))
    yp = x_prompt.reshape(b * s, d)
    ys = x_sample.reshape(db * t, d)
    outs = [[] for _ in range(10)]
    for l in range(depth):
        lam_init = 0.8 - 0.6 * math.exp(-0.3 * l)
        w = {'attn_norm_w': attn_norm_w[l], 'w_in': w_in[l], 'q_norm_w': q_norm_w[l],
             'k_norm_w': k_norm_w[l], 'xq_norm_w': xq_norm_w[l],
             'w_branch_attn': w_branch_attn[l], 'w_branch_pool': w_branch_pool[l],
             'w_branch_cross': w_branch_cross[l], 'w_out': w_out[l],
             'ffn_norm_w': ffn_norm_w[l], 'w_gate': w_gate[l], 'w_up': w_up[l],
             'conv_w': conv_w[l], 'conv_b': conv_b[l], 'w_down': w_down[l]}
        lam_vecs = [v[l].reshape(1, HEAD_DIM).astype(F32) for v in (lambda_q1, lambda_k1, lambda_q2, lambda_k2)]
        pw = pool_w[l].astype(BF16)
        ps = pool_scale[l]
        sw = subln_w[l]

        mn = _rmsnorm(mem_prompt.reshape(b * n_mem, d), mem_norm_w[l], "mem_norm")
        tm_m = _row_tile(b * n_mem, 1024)
        mk32, mkbf = _proj(mn, w_mem_k[l], 0, X_W, [F32, BF16], "proj_mem_k", tm=tm_m,
                           norm_w=xk_norm_w[l])
        mv32, mvbf = _proj(mn, w_mem_v[l], 0, X_W, [F32, BF16], "proj_mem_v", tm=tm_m)
        tm_p = _row_tile(s, 1024)
        yp, k_p, v_p, pz_p, gl_p, wbf = _layer(
            yp, w, lam_init, tm=tm_p, seq_len=s, rope=rope_p,
            attn_fn=lambda q, k, v: _attn_prompt(q, k, v, lam_vecs, sw, b, s, lam_init),
            pool_fn=lambda pz: _pool_prompt(pz, pw, ps, s),
            xattn_fn=lambda xq: _xattn_prompt(xq, mkbf, mvbf, b, s, n_mem),
            conv_prev=None, q_dtype=BF16)
        tiles_per_seq = s // tm_p
        conv_p = gl_p.reshape(b, tiles_per_seq, CONV_HALO, dff)[:, -1, CONV_HALO - (CONV_W - 1):]

        sc = state_conv[l]
        zrow = jnp.zeros((db, 1, dff), F32)
        p1 = jnp.concatenate([sc[:, 1:2]] + [zrow] * (t - 1), axis=1).reshape(db * t, dff)
        p2 = jnp.concatenate([sc[:, 0:1], sc[:, 1:2]] + [zrow] * (t - 2), axis=1).reshape(db * t, dff)
        sp = state_pool[l]
        ys, k_s, v_s, pz_s, g_s, _ = _layer(
            ys, w, lam_init, tm=_row_tile(db * t, 1024), seq_len=t, rope=rope_s,
            attn_fn=lambda q, k, v: _attn_sample(q, k, v, cache_k, cache_v, l, page_table, lam_vecs, sw, db, t,
                                                 lam_init).astype(BF16),
            pool_fn=lambda pz: _pool_sample(pz, sp, pw, ps, db, t, past_len),
            xattn_fn=lambda xq: _xattn_sample(xq, cache_mem_k, cache_mem_v, l, db, t),
            conv_prev=(p1, p2), q_dtype=F32, wbf=wbf)
        pool_s = jnp.concatenate([sp, pz_s.reshape(db, t, POOL_W)], axis=1)[:, -n_state:]
        conv_s = jnp.concatenate([sc, g_s.reshape(db, t, dff)], axis=1)[:, -(CONV_W - 1):]

        vals = [k_p.reshape(b, s, N_HEADS, V_DIM), v_p.reshape(b, s, N_HEADS, V_DIM),
                mk32.reshape(b, n_mem, X_HEADS, X_HEAD_DIM), mv32.reshape(b, n_mem, X_HEADS, X_HEAD_DIM),
                pz_p.reshape(b, s, POOL_W)[:, -n_state:], conv_p,
                k_s.reshape(db, t, N_HEADS, V_DIM), v_s.reshape(db, t, N_HEADS, V_DIM), pool_s, conv_s]
        for o, v in zip(outs, vals):
            o.append(v)
    return (yp.reshape(b, s, d), ys.reshape(db, t, d)) + tuple(jnp.stack(o) for o in outs)
```
